```python
import jax, jax.numpy as jnp
from jax import lax
import numpy as np

D_MODEL = 2048
BATCH = 2
SEQ = 4096
DEPTH = 4

HEAD_DIM = 128
POOL_GROUPS = 4
POOL_WINDOWS = (2, 4, 8, 16)
D_POOL = POOL_GROUPS * HEAD_DIM
N_ATTN_HEADS = 8
D_ATTN = N_ATTN_HEADS * HEAD_DIM
N_IDX_HEADS = 16
D_IDX = 64
TOPK_MAX = 256
Q_BLOCK = 128
GMLP_HEADS = 4
D_GMLP = GMLP_HEADS * HEAD_DIM
CHUNK = 128
D_MIX = D_POOL + D_ATTN + D_GMLP
D_IN = D_POOL + 3 * D_ATTN + N_IDX_HEADS * D_IDX + D_IDX + N_IDX_HEADS + 2 * D_GMLP
D_FF = 5632
CONV_W = 3
EPS = 1e-6

kernel_name = "hymba_style_pool_dsa_gmlp_trunk"


def rmsnorm(x, g):
    xf = x.astype(jnp.float32)
    y = xf * lax.rsqrt(jnp.mean(xf * xf, axis=-1, keepdims=True) + EPS)
    return (y * g.astype(jnp.float32)).astype(x.dtype)


def pool_mixer(a, w_pool, scale):
    B, S, _ = a.shape
    a4 = a.reshape(B, S, POOL_GROUPS, HEAD_DIM)
    csum = jnp.cumsum(a4.astype(jnp.float32), axis=1)
    pos = jnp.arange(1, S + 1, dtype=jnp.float32)
    means = []
    for g, w in enumerate(POOL_WINDOWS):
        cg = csum[:, :, g]
        prev = jnp.pad(cg, ((0, 0), (w, 0), (0, 0)))[:, :S]
        means.append((cg - prev) / jnp.minimum(pos, w)[None, :, None])
    pooled = jnp.stack(means, axis=2).astype(a.dtype) - a4
    y = jnp.einsum('bsgc,gcd->bsgd', pooled, w_pool)
    return y.reshape(B, S, D_POOL) * scale


def dsa_mixer(q, k, v, q_idx, k_idx, w_idx):
    B, S, _ = q.shape
    top_k = min(TOPK_MAX, S // 4)
    q = q.reshape(B, S, N_ATTN_HEADS, HEAD_DIM)
    k = k.reshape(B, S, N_ATTN_HEADS, HEAD_DIM)
    v = v.reshape(B, S, N_ATTN_HEADS, HEAD_DIM)
    q_idx = q_idx.reshape(B, S, N_IDX_HEADS, D_IDX)
    k_idx_f = k_idx.astype(jnp.float32)
    nb = S // Q_BLOCK

    def to_blocks(t):
        return jnp.moveaxis(t.reshape(B, nb, Q_BLOCK, *t.shape[2:]), 1, 0)

    starts = jnp.arange(nb, dtype=jnp.int32) * Q_BLOCK
    key_pos = jnp.arange(S, dtype=jnp.int32)

    def one_block(args):
        qb, qib, wb, start = args
        qpos = start + jnp.arange(Q_BLOCK, dtype=jnp.int32)
        causal = key_pos[None, :] <= qpos[:, None]
        logits = jnp.einsum('bthd,bsd->bths', qib.astype(jnp.float32), k_idx_f) * (D_IDX ** -0.5)
        wts = wb.astype(jnp.float32) * (N_IDX_HEADS ** -0.5)
        score = jnp.einsum('bth,bths->bts', wts, jax.nn.relu(logits))
        score = jnp.where(causal[None], score, -jnp.inf)
        _, idx = lax.top_k(score, top_k)
        valid = idx <= qpos[None, :, None]
        k_sel = jax.vmap(lambda kb, ib: kb[ib])(k, idx)
        v_sel = jax.vmap(lambda vb, ib: vb[ib])(v, idx)
        s = jnp.einsum('bthd,btkhd->bthk', qb, k_sel).astype(jnp.float32) * (HEAD_DIM ** -0.5)
        s = jnp.where(valid[:, :, None, :], s, -jnp.inf)
        p = jax.nn.softmax(s, axis=-1).astype(v.dtype)
        return jnp.einsum('bthk,btkhd->bthd', p, v_sel)

    out = lax.map(one_block, (to_blocks(q), to_blocks(q_idx), to_blocks(w_idx), starts))
    return jnp.moveaxis(out, 0, 1).reshape(B, S, D_ATTN)


def gmlp_mixer(z, ln_g, ln_b, w_s, b_s):
    B, S, _ = z.shape
    z = jax.nn.gelu(z)
    u, v = jnp.split(z, 2, axis=-1)
    vf = v.astype(jnp.float32)
    mu = jnp.mean(vf, axis=-1, keepdims=True)
    var = jnp.mean(jnp.square(vf - mu), axis=-1, keepdims=True)
    v = ((vf - mu) * lax.rsqrt(var + EPS) * ln_g.astype(jnp.float32) + ln_b.astype(jnp.float32)).astype(z.dtype)
    v = v.reshape(B, S // CHUNK, CHUNK, GMLP_HEADS, HEAD_DIM)
    mask = jnp.tril(jnp.ones((CHUNK, CHUNK), dtype=bool))
    ws = jnp.where(mask[None], w_s, 0.0)
    mixed = jnp.einsum('gts,bnsgc->bntgc', ws, v) + b_s.T[None, None, :, :, None]
    return u * mixed.reshape(B, S, D_GMLP)


def conv_ffn(h, w_up, conv_w, conv_b, w_down):
    S = h.shape[1]
    up = h @ w_up
    upp = jnp.pad(up, ((0, 0), (CONV_W - 1, 0), (0, 0)))
    conv = conv_b + conv_w[0] * upp[:, 0:S]
    for j in range(1, CONV_W):
        conv = conv + conv_w[j] * upp[:, j:j + S]
    gate, val = jnp.split(conv, 2, axis=-1)
    return (jax.nn.silu(gate) * val) @ w_down


def setup_inputs(seed: int = 0) -> dict:
    key = jax.random.key(seed)
    ks = jax.random.split(key, 18)
    f32 = jnp.float32
    nrm = lambda k, shape, s: jax.random.normal(k, shape, f32) * s
    return {
        "x": nrm(ks[0], (BATCH, SEQ, D_MODEL), 1.0),
        "w_in": nrm(ks[1], (DEPTH, D_MODEL, D_IN), D_MODEL ** -0.5),
        "pool_w": nrm(ks[2], (DEPTH, POOL_GROUPS, HEAD_DIM, HEAD_DIM), HEAD_DIM ** -0.5),
        "pool_scale": 1.0 + nrm(ks[3], (DEPTH, D_POOL), 0.02),
        "sgu_ln_g": 1.0 + nrm(ks[4], (DEPTH, D_GMLP), 0.02),
        "sgu_ln_b": nrm(ks[5], (DEPTH, D_GMLP), 0.02),
        "sgu_w": nrm(ks[6], (DEPTH, GMLP_HEADS, CHUNK, CHUNK), 0.5 * CHUNK ** -0.5),
        "sgu_b": 1.0 + nrm(ks[7], (DEPTH, GMLP_HEADS, CHUNK), 0.02),
        "w_out": nrm(ks[8], (DEPTH, D_MIX, D_MODEL), D_MIX ** -0.5),
        "ffn_w_up": nrm(ks[9], (DEPTH, D_MODEL, 2 * D_FF), D_MODEL ** -0.5),
        "ffn_conv_w": nrm(ks[10], (DEPTH, CONV_W, 2 * D_FF), CONV_W ** -0.5),
        "ffn_conv_b": nrm(ks[11], (DEPTH, 2 * D_FF), 0.01),
        "ffn_w_down": nrm(ks[12], (DEPTH, D_FF, D_MODEL), D_FF ** -0.5),
        "norm_pre_mix": 1.0 + nrm(ks[13], (DEPTH, D_MODEL), 0.02),
        "norm_post_mix": 1.0 + nrm(ks[14], (DEPTH, D_MODEL), 0.02),
        "norm_pre_ffn": 1.0 + nrm(ks[15], (DEPTH, D_MODEL), 0.02),
        "norm_post_ffn": 1.0 + nrm(ks[16], (DEPTH, D_MODEL), 0.02),
    }


def reference(x, w_in, pool_w, pool_scale, sgu_ln_g, sgu_ln_b, sgu_w, sgu_b, w_out,
              ffn_w_up, ffn_conv_w, ffn_conv_b, ffn_w_down,
              norm_pre_mix, norm_post_mix, norm_pre_ffn, norm_post_ffn):
    sizes = (D_POOL, D_ATTN, D_ATTN, D_ATTN, N_IDX_HEADS * D_IDX, D_IDX, N_IDX_HEADS, 2 * D_GMLP)
    points = []
    acc = 0
    for s in sizes[:-1]:
        acc += s
        points.append(acc)
    for l in range(DEPTH):
        h = rmsnorm(x, norm_pre_mix[l])
        z = h @ w_in[l]
        z_pool, q, k, v, q_idx, k_idx, w_idx, z_gmlp = jnp.split(z, points, axis=-1)
        y_pool = pool_mixer(z_pool, pool_w[l], pool_scale[l])
        y_attn = dsa_mixer(q, k, v, q_idx, k_idx, w_idx)
        y_gmlp = gmlp_mixer(z_gmlp, sgu_ln_g[l], sgu_ln_b[l], sgu_w[l], sgu_b[l])
        mix = jnp.concatenate([y_pool, y_attn, y_gmlp], axis=-1) @ w_out[l]
        x = x + rmsnorm(mix, norm_post_mix[l])
        h = rmsnorm(x, norm_pre_ffn[l])
        f = conv_ffn(h, ffn_w_up[l], ffn_conv_w[l], ffn_conv_b[l], ffn_w_down[l])
        x = x + rmsnorm(f, norm_post_ffn[l])
    return x
```

```python
import functools

import jax
import jax.numpy as jnp
from jax import lax
from jax.experimental import pallas as pl
from jax.experimental.pallas import tpu as pltpu

F32 = jnp.float32
BF16 = jnp.bfloat16

D_MODEL = 2048
HEAD_DIM = 128
POOL_WINDOWS = (2, 4, 8, 16)
D_POOL = len(POOL_WINDOWS) * HEAD_DIM
POOL_HALO = 16
N_ATTN_HEADS = 8
D_ATTN = N_ATTN_HEADS * HEAD_DIM
N_IDX_HEADS = 16
D_IDX = 64
TOPK_MAX = 256
GMLP_HEADS = 4
D_GMLP = GMLP_HEADS * HEAD_DIM
CHUNK = 128
D_FF = 5632
CONV_W = 3
CONV_HALO = 8
EPS = 1e-6

LANES = 128
NEG_BIAS = -1e30

ZB_WIDTH = 4 * D_ATTN + 2 * LANES
ZB_Q, ZB_K, ZB_V, ZB_QI = 0, 1, 2, 3
ZB_KA = 4 * D_ATTN // LANES
ZB_KB = ZB_KA + 1
ZF_WIDTH = D_POOL + 2 * D_GMLP + LANES
ZF_WI = (D_POOL + 2 * D_GMLP) // LANES

VMEM_LIMIT = 56 * 1024 * 1024


def _cparams(sem):
    return pltpu.CompilerParams(dimension_semantics=sem, vmem_limit_bytes=VMEM_LIMIT)


def _rms(x, g):
    ms = jnp.mean(x * x, axis=-1, keepdims=True)
    return x * lax.rsqrt(ms + EPS) * g


def _norm_matmul_kernel(x_ref, g_ref, w_ref, o_ref, h_ref):
    @pl.when(pl.program_id(1) == 0)
    def _():
        h_ref[...] = _rms(x_ref[...], g_ref[...]).astype(BF16)

    o_ref[...] = jnp.dot(h_ref[...], w_ref[...], preferred_element_type=F32).astype(o_ref.dtype)


def _norm_matmul(x, g, w, out_dtype, tm, tn):
    m, d = x.shape
    n = w.shape[1]
    return pl.pallas_call(
        _norm_matmul_kernel,
        grid=(m // tm, n // tn),
        in_specs=[
            pl.BlockSpec((tm, d), lambda i, j: (i, 0)),
            pl.BlockSpec((1, d), lambda i, j: (0, 0)),
            pl.BlockSpec((d, tn), lambda i, j: (0, j)),
        ],
        out_specs=pl.BlockSpec((tm, tn), lambda i, j: (i, j)),
        out_shape=jax.ShapeDtypeStruct((m, n), out_dtype),
        scratch_shapes=[pltpu.VMEM((tm, d), BF16)],
        compiler_params=_cparams(("parallel", "arbitrary")),
        name="norm_in_proj",
    )(x, g, w)


def _gelu_tanh(x):
    return 0.5 * x * (1.0 + jnp.tanh(0.7978845608028654 * (x + 0.044715 * (x * x * x))))


def _mixer_kernel(a_ref, halo_ref, u_ref, v_ref, pw_ref, ps_ref, lg_ref, lb_ref, sw_ref, sb_ref,
                  o_ref, ext_ref, *, seq):
    tt = a_ref.shape[0]
    t0 = (pl.program_id(0) * tt) % seq
    a = a_ref[...]
    ext_ref[0:POOL_HALO, :] = jnp.where(t0 == 0, 0.0, halo_ref[...])
    ext_ref[POOL_HALO:, :] = a
    pos = (t0 + 1 + lax.broadcasted_iota(jnp.int32, (tt, 1), 0)).astype(F32)
    for g, w in enumerate(POOL_WINDOWS):
        cs = slice(g * HEAD_DIM, (g + 1) * HEAD_DIM)
        ag = a[:, cs]
        acc = ag
        for j in range(1, w):
            acc = acc + ext_ref[pl.ds(POOL_HALO - j, tt), cs]
        pooled = acc / jnp.minimum(pos, float(w)) - ag
        y = jnp.dot(pooled.astype(BF16), pw_ref[g], preferred_element_type=F32) * ps_ref[:, cs]
        o_ref[:, cs] = y.astype(o_ref.dtype)

    zu = _gelu_tanh(u_ref[...])
    zv = _gelu_tanh(v_ref[...])
    mu = jnp.mean(zv, axis=-1, keepdims=True)
    dv = zv - mu
    var = jnp.mean(dv * dv, axis=-1, keepdims=True)
    vn = (dv * lax.rsqrt(var + EPS) * lg_ref[...] + lb_ref[...]).astype(BF16)
    tril = (lax.broadcasted_iota(jnp.int32, (CHUNK, CHUNK), 0)
            >= lax.broadcasted_iota(jnp.int32, (CHUNK, CHUNK), 1))
    for g in range(GMLP_HEADS):
        cs = slice(g * HEAD_DIM, (g + 1) * HEAD_DIM)
        wg = jnp.where(tril, sw_ref[g], 0.0).astype(BF16)
        for c in range(tt // CHUNK):
            rs = slice(c * CHUNK, (c + 1) * CHUNK)
            mixed = jnp.dot(wg, vn[rs, cs], preferred_element_type=F32) + sb_ref[g]
            o_ref[rs, D_POOL + g * HEAD_DIM:D_POOL + (g + 1) * HEAD_DIM] = (zu[rs, cs] * mixed).astype(o_ref.dtype)


def _mixer(zf, pool_w, pool_scale, ln_g, ln_b, sgu_w, sgu_b_exp, seq, tt):
    m = zf.shape[0]
    hb = tt // POOL_HALO
    const3 = lambda i: (0, 0, 0)
    const2 = lambda i: (0, 0)
    return pl.pallas_call(
        functools.partial(_mixer_kernel, seq=seq),
        grid=(m // tt,),
        in_specs=[
            pl.BlockSpec((tt, D_POOL), lambda i: (i, 0)),
            pl.BlockSpec((POOL_HALO, D_POOL), lambda i: (jnp.maximum(i * hb - 1, 0), 0)),
            pl.BlockSpec((tt, D_GMLP), lambda i: (i, D_POOL // D_GMLP)),
            pl.BlockSpec((tt, D_GMLP), lambda i: (i, D_POOL // D_GMLP + 1)),
            pl.BlockSpec((len(POOL_WINDOWS), HEAD_DIM, HEAD_DIM), const3),
            pl.BlockSpec((1, D_POOL), const2),
            pl.BlockSpec((1, D_GMLP), const2),
            pl.BlockSpec((1, D_GMLP), const2),
            pl.BlockSpec((GMLP_HEADS, CHUNK, CHUNK), const3),
            pl.BlockSpec((GMLP_HEADS, CHUNK, HEAD_DIM), const3),
        ],
        out_specs=pl.BlockSpec((tt, D_POOL + D_GMLP), lambda i: (i, 0)),
        out_shape=jax.ShapeDtypeStruct((m, D_POOL + D_GMLP), BF16),
        scratch_shapes=[pltpu.VMEM((POOL_HALO + tt, D_POOL), F32)],
        compiler_params=_cparams(("parallel",)),
        name="pool_gmlp_mixer",
    )(zf, zf, zf, zf, pool_w, pool_scale, ln_g, ln_b, sgu_w, sgu_b_exp)


BISECT_ITERS = 32


def _fold_lanes(x):
    acc = x[:, 0:LANES]
    for j in range(1, x.shape[1] // LANES):
        acc = acc + x[:, j * LANES:(j + 1) * LANES]
    return acc


def _indexer_kernel(qi_ref, ka_ref, kb_ref, wi_ref, o_ref, score_ref, wb_ref, jstar_ref, *, top_k, seq):
    nkc, tq, kc = score_ref.shape
    i = pl.program_id(1)
    nk = ((i + 1) * tq + kc - 1) // kc
    kf = float(top_k)

    wts = wi_ref[:, 0:N_IDX_HEADS] * (N_IDX_HEADS ** -0.5 * D_IDX ** -0.5)
    for h in range(N_IDX_HEADS):
        wb_ref[h] = jnp.broadcast_to(wts[:, h:h + 1], (tq, LANES))

    row_pos = i * tq + lax.broadcasted_iota(jnp.int32, (tq, 1), 0)
    nt = (((1,), (1,)), ((), ()))

    def score_chunk(c, carry):
        mn, mx = carry
        k0 = pl.multiple_of(c * kc, kc)
        ka = ka_ref[pl.ds(k0, kc), :]
        kb = kb_ref[pl.ds(k0, kc), :]
        cols = [jnp.zeros((tq, LANES), F32) for _ in range(kc // LANES)]
        for p in range(N_IDX_HEADS // 2):
            qp = qi_ref[:, p * LANES:(p + 1) * LANES]
            for half, kk in enumerate((ka, kb)):
                lg = lax.dot_general(qp, kk, nt, preferred_element_type=F32)
                wbh = wb_ref[2 * p + half]
                for j in range(kc // LANES):
                    cols[j] = cols[j] + jnp.maximum(lg[:, j * LANES:(j + 1) * LANES], 0.0) * wbh
        sc = jnp.concatenate(cols, axis=1)
        causal = (k0 + lax.broadcasted_iota(jnp.int32, (1, kc), 1)) <= row_pos
        score_ref[c] = jnp.where(causal, sc, -jnp.inf)
        mx = jnp.maximum(mx, jnp.max(jnp.where(causal, sc, -jnp.inf), axis=-1, keepdims=True))
        mn = jnp.minimum(mn, jnp.min(jnp.where(causal, sc, jnp.inf), axis=-1, keepdims=True))
        return mn, mx

    lo, hi = lax.fori_loop(0, nk, score_chunk,
                           (jnp.full((tq, 1), jnp.inf, F32), jnp.full((tq, 1), -jnp.inf, F32)))

    def count(pred):
        def body(c, acc):
            col = c * kc + lax.broadcasted_iota(jnp.int32, (1, kc), 1)
            return acc + _fold_lanes(jnp.where(pred(score_ref[c], col), 1.0, 0.0))
        acc = lax.fori_loop(0, nk, body, jnp.zeros((tq, LANES), F32))
        return jnp.sum(acc, axis=-1, keepdims=True)

    def bisect(_, carry):
        lo, hi = carry
        mid = 0.5 * lo + 0.5 * hi
        ge = count(lambda s, col: s >= mid) >= kf
        return jnp.where(ge, mid, lo), jnp.where(ge, hi, mid)

    lo, hi = lax.fori_loop(0, BISECT_ITERS, bisect, (lo, hi))

    c_hi = count(lambda s, col: s >= hi)
    c_lo = count(lambda s, col: s >= lo)
    at_max = c_hi >= kf
    thr = jnp.where(at_max, hi, lo)
    upper = jnp.where(at_max, jnp.inf, hi)
    c_gt = jnp.where(at_max, 0.0, c_hi)
    c_thr = jnp.where(at_max, c_hi, c_lo)
    need = kf - c_gt
    excess = c_thr > kf

    jstar_ref[...] = jnp.full((tq, LANES), seq, jnp.int32)

    @pl.when(jnp.max(jnp.where(excess, 1.0, 0.0)) > 0.0)
    def _():
        def jbisect(_, carry):
            jlo, jhi = carry
            jmid = (jlo + jhi) >> 1
            ge = count(lambda s, col: (s >= thr) & (s < upper) & (col <= jmid)) >= need
            return jnp.where(ge, jlo, jmid), jnp.where(ge, jmid, jhi)

        n_iter = seq.bit_length()
        _, jhi = lax.fori_loop(0, n_iter, jbisect,
                               (jnp.full((tq, 1), -1, jnp.int32), jnp.full((tq, 1), seq - 1, jnp.int32)))
        jstar_ref[...] = jnp.broadcast_to(jnp.where(excess, jhi, seq), (tq, LANES))

    jstar = jstar_ref[:, 0:1]

    def emit(c, _):
        s = score_ref[c]
        col = c * kc + lax.broadcasted_iota(jnp.int32, (1, kc), 1)
        sel = (s >= thr) & ((s >= upper) | (col <= jstar))
        o_ref[c] = jnp.where(sel, 0.0, NEG_BIAS).astype(o_ref.dtype)
        return 0

    lax.fori_loop(0, nk, emit, 0)

    def fill(c, _):
        o_ref[c] = jnp.full((tq, kc), NEG_BIAS, o_ref.dtype)
        return 0

    lax.fori_loop(nk, nkc, fill, 0)


def _indexer(zb, zf, batch, seq, tq, kc):
    nq = seq // tq
    nkc = seq // kc
    top_k = min(TOPK_MAX, seq // 4)
    return pl.pallas_call(
        functools.partial(_indexer_kernel, top_k=top_k, seq=seq),
        grid=(batch, nq),
        in_specs=[
            pl.BlockSpec((tq, D_ATTN), lambda b, i: (b * nq + i, ZB_QI)),
            pl.BlockSpec((seq, LANES), lambda b, i: (b, ZB_KA)),
            pl.BlockSpec((seq, LANES), lambda b, i: (b, ZB_KB)),
            pl.BlockSpec((tq, LANES), lambda b, i: (b * nq + i, ZF_WI)),
        ],
        out_specs=pl.BlockSpec((None, nkc, tq, kc), lambda b, i: (b * nq + i, 0, 0, 0)),
        out_shape=jax.ShapeDtypeStruct((batch * nq, nkc, tq, kc), BF16),
        scratch_shapes=[
            pltpu.VMEM((nkc, tq, kc), F32),
            pltpu.VMEM((N_IDX_HEADS, tq, LANES), F32),
            pltpu.VMEM((tq, LANES), jnp.int32),
        ],
        compiler_params=_cparams(("parallel", "arbitrary")),
        name="dsa_indexer_topk",
    )(zb, zb, zb, zf)


def _attn_kernel(q_ref, k_ref, v_ref, b_ref, o_ref, m_ref, l_ref, acc_ref):
    tq = q_ref.shape[0]
    kc = k_ref.shape[0]
    i = pl.program_id(1)
    c = pl.program_id(2)
    last = ((i + 1) * tq - 1) // kc
    nt = (((1,), (1,)), ((), ()))

    @pl.when(c == 0)
    def _():
        m_ref[...] = jnp.full(m_ref.shape, -jnp.inf, F32)
        l_ref[...] = jnp.zeros(l_ref.shape, F32)
        acc_ref[...] = jnp.zeros(acc_ref.shape, F32)

    @pl.when(c <= last)
    def _():
        bias = b_ref[...].astype(F32)
        for h in range(N_ATTN_HEADS):
            cs = slice(h * HEAD_DIM, (h + 1) * HEAD_DIM)
            s = lax.dot_general(q_ref[:, cs], k_ref[:, cs], nt, preferred_element_type=F32)
            s = s * (HEAD_DIM ** -0.5) + bias
            m_prev = m_ref[h]
            m_new = jnp.maximum(m_prev, jnp.max(s, axis=-1, keepdims=True))
            alpha = jnp.exp(m_prev - m_new)
            p = jnp.exp(s - m_new[:, 0:1])
            l_ref[h] = alpha * l_ref[h] + jnp.sum(p, axis=-1, keepdims=True)
            acc_ref[:, cs] = alpha * acc_ref[:, cs] + jnp.dot(p.astype(BF16), v_ref[:, cs],
                                                              preferred_element_type=F32)
            m_ref[h] = m_new

    @pl.when(c == pl.num_programs(2) - 1)
    def _():
        for h in range(N_ATTN_HEADS):
            cs = slice(h * HEAD_DIM, (h + 1) * HEAD_DIM)
            o_ref[:, cs] = (acc_ref[:, cs] / l_ref[h]).astype(o_ref.dtype)


def _attention(zb, bias, batch, seq, tq, kc):
    nq = seq // tq
    nkc = seq // kc
    nkb = seq // kc

    def kv_idx(col):
        def f(b, i, c):
            return (b * nkb + jnp.minimum(c, ((i + 1) * tq - 1) // kc), col)
        return f

    return pl.pallas_call(
        _attn_kernel,
        grid=(batch, nq, nkc),
        in_specs=[
            pl.BlockSpec((tq, D_ATTN), lambda b, i, c: (b * nq + i, ZB_Q)),
            pl.BlockSpec((kc, D_ATTN), kv_idx(ZB_K)),
            pl.BlockSpec((kc, D_ATTN), kv_idx(ZB_V)),
            pl.BlockSpec((None, None, tq, kc),
                         lambda b, i, c: (b * nq + i, jnp.minimum(c, ((i + 1) * tq - 1) // kc), 0, 0)),
        ],
        out_specs=pl.BlockSpec((tq, D_ATTN), lambda b, i, c: (b * nq + i, 0)),
        out_shape=jax.ShapeDtypeStruct((batch * seq, D_ATTN), BF16),
        scratch_shapes=[
            pltpu.VMEM((N_ATTN_HEADS, tq, LANES), F32),
            pltpu.VMEM((N_ATTN_HEADS, tq, LANES), F32),
            pltpu.VMEM((tq, D_ATTN), F32),
        ],
        compiler_params=_cparams(("parallel", "parallel", "arbitrary")),
        name="dsa_masked_attention",
    )(zb, zb, zb, bias)


def _out_proj_kernel(ypg_ref, ya_ref, wpg_ref, wa_ref, x_ref, g_ref, o_ref):
    mix = jnp.dot(ypg_ref[...], wpg_ref[...], preferred_element_type=F32)
    mix = mix + jnp.dot(ya_ref[...], wa_ref[...], preferred_element_type=F32)
    o_ref[...] = x_ref[...] + _rms(mix, g_ref[...])


def _out_proj(ypg, ya, wpg, wa, x, g, tm):
    m, d = x.shape
    return pl.pallas_call(
        _out_proj_kernel,
        grid=(m // tm,),
        in_specs=[
            pl.BlockSpec((tm, ypg.shape[1]), lambda i: (i, 0)),
            pl.BlockSpec((tm, ya.shape[1]), lambda i: (i, 0)),
            pl.BlockSpec(wpg.shape, lambda i: (0, 0)),
            pl.BlockSpec(wa.shape, lambda i: (0, 0)),
            pl.BlockSpec((tm, d), lambda i: (i, 0)),
            pl.BlockSpec((1, d), lambda i: (0, 0)),
        ],
        out_specs=pl.BlockSpec((tm, d), lambda i: (i, 0)),
        out_shape=jax.ShapeDtypeStruct((m, d), F32),
        compiler_params=_cparams(("parallel",)),
        name="out_proj_residual",
    )(ypg, ya, wpg, wa, x, g)


def _ffn_kernel(x_ref, xh_ref, g1_ref, wg_ref, wv_ref, cwg_ref, cwv_ref, cbg_ref, cbv_ref, wd_ref, g2_ref,
                o_ref, h_ref, up_ref, acc_ref, *, seq):
    tm = x_ref.shape[0]
    tf = wg_ref.shape[1]
    j = pl.program_id(1)
    at_start = (pl.program_id(0) * tm) % seq == 0

    @pl.when(j == 0)
    def _():
        h_ref[0:CONV_HALO, :] = _rms(xh_ref[...], g1_ref[...]).astype(BF16)
        h_ref[CONV_HALO:, :] = _rms(x_ref[...], g1_ref[...]).astype(BF16)
        acc_ref[...] = jnp.zeros(acc_ref.shape, F32)

    def conv(w_ref, cw_ref, cb_ref):
        up = jnp.dot(h_ref[...], w_ref[...], preferred_element_type=F32)
        up_ref[0:CONV_HALO, :] = jnp.where(at_start, 0.0, up[0:CONV_HALO, :])
        up_ref[CONV_HALO:, :] = up[CONV_HALO:, :]
        out = cb_ref[...] + cw_ref[CONV_W - 1:CONV_W, :] * up[CONV_HALO:, :]
        for t in range(CONV_W - 1):
            shift = CONV_W - 1 - t
            out = out + cw_ref[t:t + 1, :] * up_ref[pl.ds(CONV_HALO - shift, tm), :]
        return out

    gate = conv(wg_ref, cwg_ref, cbg_ref)
    act = gate * (1.0 / (1.0 + jnp.exp(-gate)))
    val = conv(wv_ref, cwv_ref, cbv_ref)
    acc_ref[...] += jnp.dot((act * val).astype(BF16), wd_ref[...], preferred_element_type=F32)

    @pl.when(j == pl.num_programs(1) - 1)
    def _():
        o_ref[...] = x_ref[...] + _rms(acc_ref[...], g2_ref[...])


def _conv_ffn(x, g1, w_up, conv_w, conv_b, w_down, g2, seq, tm, tf):
    m, d = x.shape
    nf = D_FF // tf
    hb = tm // CONV_HALO
    return pl.pallas_call(
        functools.partial(_ffn_kernel, seq=seq),
        grid=(m // tm, nf),
        in_specs=[
            pl.BlockSpec((tm, d), lambda i, j: (i, 0)),
            pl.BlockSpec((CONV_HALO, d), lambda i, j: (jnp.maximum(i * hb - 1, 0), 0)),
            pl.BlockSpec((1, d), lambda i, j: (0, 0)),
            pl.BlockSpec((d, tf), lambda i, j: (0, j)),
            pl.BlockSpec((d, tf), lambda i, j: (0, nf + j)),
            pl.BlockSpec((CONV_W, tf), lambda i, j: (0, j)),
            pl.BlockSpec((CONV_W, tf), lambda i, j: (0, nf + j)),
            pl.BlockSpec((1, tf), lambda i, j: (0, j)),
            pl.BlockSpec((1, tf), lambda i, j: (0, nf + j)),
            pl.BlockSpec((tf, d), lambda i, j: (j, 0)),
            pl.BlockSpec((1, d), lambda i, j: (0, 0)),
        ],
        out_specs=pl.BlockSpec((tm, d), lambda i, j: (i, 0)),
        out_shape=jax.ShapeDtypeStruct((m, d), F32),
        scratch_shapes=[
            pltpu.VMEM((CONV_HALO + tm, d), BF16),
            pltpu.VMEM((CONV_HALO + tm, tf), F32),
            pltpu.VMEM((tm, d), F32),
        ],
        compiler_params=_cparams(("parallel", "arbitrary")),
        name="conv_ffn_residual",
    )(x, x, g1, w_up, w_up, conv_w, conv_w, conv_b, conv_b, w_down, g2)


def _split_w_in(w):
    o = 0
    parts = {}
    for name, size in (("pool", D_POOL), ("q", D_ATTN), ("k", D_ATTN), ("v", D_ATTN),
                       ("qi", N_IDX_HEADS * D_IDX), ("ki", D_IDX), ("wi", N_IDX_HEADS), ("gmlp", 2 * D_GMLP)):
        parts[name] = w[:, o:o + size]
        o += size
    zk = jnp.zeros_like(parts["ki"])
    wb = jnp.concatenate([parts["q"], parts["k"], parts["v"], parts["qi"], parts["ki"], zk, zk, parts["ki"]], axis=1)
    zw = jnp.zeros((w.shape[0], LANES - N_IDX_HEADS), w.dtype)
    wf = jnp.concatenate([parts["pool"], parts["gmlp"], parts["wi"], zw], axis=1)
    return wb.astype(BF16), wf.astype(BF16)


def _trunk(x, w_in, pool_w, pool_scale, sgu_ln_g, sgu_ln_b, sgu_w, sgu_b, w_out, ffn_w_up, ffn_conv_w,
           ffn_conv_b, ffn_w_down, norm_pre_mix, norm_post_mix, norm_pre_ffn, norm_post_ffn,
           *, tm, tq, kc, tf):
    batch, seq, d = x.shape
    depth = w_in.shape[0]
    xf = x.reshape(batch * seq, d)
    row = lambda v: v.reshape(1, -1)
    for l in range(depth):
        wb, wf = _split_w_in(w_in[l])
        g = row(norm_pre_mix[l])
        zb = _norm_matmul(xf, g, wb, BF16, tm, ZB_WIDTH // 2)
        zf = _norm_matmul(xf, g, wf, F32, tm, ZF_WIDTH)
        sb_exp = jnp.broadcast_to(sgu_b[l][:, :, None], (GMLP_HEADS, CHUNK, HEAD_DIM))
        ypg = _mixer(zf, pool_w[l].astype(BF16), row(pool_scale[l]), row(sgu_ln_g[l]), row(sgu_ln_b[l]),
                     sgu_w[l], sb_exp, seq, tm)
        bias = _indexer(zb, zf, batch, seq, tq, kc)
        ya = _attention(zb, bias, batch, seq, tq, kc)
        wo = w_out[l].astype(BF16)
        wpg = jnp.concatenate([wo[0:D_POOL], wo[D_POOL + D_ATTN:]], axis=0)
        wa = wo[D_POOL:D_POOL + D_ATTN]
        xf = _out_proj(ypg, ya, wpg, wa, xf, row(norm_post_mix[l]), tm)
        xf = _conv_ffn(xf, row(norm_pre_ffn[l]), ffn_w_up[l].astype(BF16), ffn_conv_w[l], row(ffn_conv_b[l]),
                       ffn_w_down[l].astype(BF16), row(norm_post_ffn[l]), seq, tm, tf)
    return xf.reshape(batch, seq, d)


def kernel(x, w_in, pool_w, pool_scale, sgu_ln_g, sgu_ln_b, sgu_w, sgu_b, w_out, ffn_w_up, ffn_conv_w,
           ffn_conv_b, ffn_w_down, norm_pre_mix, norm_post_mix, norm_pre_ffn, norm_post_ffn):
    return _trunk(x, w_in, pool_w, pool_scale, sgu_ln_g, sgu_ln_b, sgu_w, sgu_b, w_out, ffn_w_up, ffn_conv_w,
                  ffn_conv_b, ffn_w_down, norm_pre_mix, norm_post_mix, norm_pre_ffn, norm_post_ffn,
                  tm=512, tq=256, kc=512, tf=512)
```

```python
import functools

import jax
import jax.numpy as jnp
from jax import lax
from jax.experimental import pallas as pl
from jax.experimental.pallas import tpu as pltpu

F32 = jnp.float32
BF16 = jnp.bfloat16

D_MODEL = 2048
HEAD_DIM = 128
POOL_WINDOWS = (2, 4, 8, 16)
D_POOL = len(POOL_WINDOWS) * HEAD_DIM
POOL_HALO = 16
N_ATTN_HEADS = 8
D_ATTN = N_ATTN_HEADS * HEAD_DIM
N_IDX_HEADS = 16
D_IDX = 64
TOPK_MAX = 256
GMLP_HEADS = 4
D_GMLP = GMLP_HEADS * HEAD_DIM
CHUNK = 128
D_FF = 5632
CONV_W = 3
CONV_HALO = 8
EPS = 1e-6

LANES = 128
NEG_BIAS = -1e30

ZB_WIDTH = 4 * D_ATTN + 2 * LANES
ZB_Q, ZB_K, ZB_V, ZB_QI = 0, 1, 2, 3
ZB_KA = 4 * D_ATTN // LANES
ZB_KB = ZB_KA + 1
ZF_WIDTH = D_POOL + 2 * D_GMLP + LANES
ZF_WI = (D_POOL + 2 * D_GMLP) // LANES

VMEM_LIMIT = 56 * 1024 * 1024


def _cparams(sem):
    return pltpu.CompilerParams(dimension_semantics=sem, vmem_limit_bytes=VMEM_LIMIT)


def _rms(x, g):
    ms = jnp.mean(x * x, axis=-1, keepdims=True)
    return x * lax.rsqrt(ms + EPS) * g


def _norm_matmul_kernel(x_ref, g_ref, w_ref, o_ref, h_ref):
    @pl.when(pl.program_id(1) == 0)
    def _():
        h_ref[...] = _rms(x_ref[...], g_ref[...]).astype(BF16)

    o_ref[...] = jnp.dot(h_ref[...], w_ref[...], preferred_element_type=F32).astype(o_ref.dtype)


def _norm_matmul(x, g, w, layer, out_dtype, tm, tn):
    m, d = x.shape
    n = w.shape[2]
    return pl.pallas_call(
        _norm_matmul_kernel,
        grid=(m // tm, n // tn),
        in_specs=[
            pl.BlockSpec((tm, d), lambda i, j: (i, 0)),
            pl.BlockSpec((1, d), lambda i, j: (0, 0)),
            pl.BlockSpec((None, d, tn), lambda i, j: (layer, 0, j)),
        ],
        out_specs=pl.BlockSpec((tm, tn), lambda i, j: (i, j)),
        out_shape=jax.ShapeDtypeStruct((m, n), out_dtype),
        scratch_shapes=[pltpu.VMEM((tm, d), BF16)],
        compiler_params=_cparams(("parallel", "arbitrary")),
        name="norm_in_proj",
    )(x, g, w)


def _gelu_tanh(x):
    return 0.5 * x * (1.0 + jnp.tanh(0.7978845608028654 * (x + 0.044715 * (x * x * x))))


def _mixer_kernel(a_ref, halo_ref, u_ref, v_ref, pw_ref, ps_ref, lg_ref, lb_ref, sw_ref, sb_ref,
                  o_ref, ext_ref, *, seq):
    tt = a_ref.shape[0]
    t0 = (pl.program_id(0) * tt) % seq
    a = a_ref[...]
    ext_ref[0:POOL_HALO, :] = jnp.where(t0 == 0, 0.0, halo_ref[...])
    ext_ref[POOL_HALO:, :] = a
    pos = (t0 + 1 + lax.broadcasted_iota(jnp.int32, (tt, 1), 0)).astype(F32)
    for g, w in enumerate(POOL_WINDOWS):
        cs = slice(g * HEAD_DIM, (g + 1) * HEAD_DIM)
        ag = a[:, cs]
        acc = ag
        for j in range(1, w):
            acc = acc + ext_ref[pl.ds(POOL_HALO - j, tt), cs]
        pooled = acc / jnp.minimum(pos, float(w)) - ag
        y = jnp.dot(pooled.astype(BF16), pw_ref[g], preferred_element_type=F32) * ps_ref[:, cs]
        o_ref[:, cs] = y.astype(o_ref.dtype)

    zu = _gelu_tanh(u_ref[...])
    zv = _gelu_tanh(v_ref[...])
    mu = jnp.mean(zv, axis=-1, keepdims=True)
    dv = zv - mu
    var = jnp.mean(dv * dv, axis=-1, keepdims=True)
    vn = (dv * lax.rsqrt(var + EPS) * lg_ref[...] + lb_ref[...]).astype(BF16)
    tril = (lax.broadcasted_iota(jnp.int32, (CHUNK, CHUNK), 0)
            >= lax.broadcasted_iota(jnp.int32, (CHUNK, CHUNK), 1))
    for g in range(GMLP_HEADS):
        cs = slice(g * HEAD_DIM, (g + 1) * HEAD_DIM)
        wg = jnp.where(tril, sw_ref[g], 0.0).astype(BF16)
        for c in range(tt // CHUNK):
            rs = slice(c * CHUNK, (c + 1) * CHUNK)
            mixed = jnp.dot(wg, vn[rs, cs], preferred_element_type=F32) + sb_ref[g]
            o_ref[rs, D_POOL + g * HEAD_DIM:D_POOL + (g + 1) * HEAD_DIM] = (zu[rs, cs] * mixed).astype(o_ref.dtype)


def _mixer(zf, pool_w, pool_scale, ln_g, ln_b, sgu_w, sgu_b_exp, seq, tt):
    m = zf.shape[0]
    hb = tt // POOL_HALO
    const3 = lambda i: (0, 0, 0)
    const2 = lambda i: (0, 0)
    return pl.pallas_call(
        functools.partial(_mixer_kernel, seq=seq),
        grid=(m // tt,),
        in_specs=[
            pl.BlockSpec((tt, D_POOL), lambda i: (i, 0)),
            pl.BlockSpec((POOL_HALO, D_POOL), lambda i: (jnp.maximum(i * hb - 1, 0), 0)),
            pl.BlockSpec((tt, D_GMLP), lambda i: (i, D_POOL // D_GMLP)),
            pl.BlockSpec((tt, D_GMLP), lambda i: (i, D_POOL // D_GMLP + 1)),
            pl.BlockSpec((len(POOL_WINDOWS), HEAD_DIM, HEAD_DIM), const3),
            pl.BlockSpec((1, D_POOL), const2),
            pl.BlockSpec((1, D_GMLP), const2),
            pl.BlockSpec((1, D_GMLP), const2),
            pl.BlockSpec((GMLP_HEADS, CHUNK, CHUNK), const3),
            pl.BlockSpec((GMLP_HEADS, CHUNK, HEAD_DIM), const3),
        ],
        out_specs=pl.BlockSpec((tt, D_POOL + D_GMLP), lambda i: (i, 0)),
        out_shape=jax.ShapeDtypeStruct((m, D_POOL + D_GMLP), BF16),
        scratch_shapes=[pltpu.VMEM((POOL_HALO + tt, D_POOL), F32)],
        compiler_params=_cparams(("parallel",)),
        name="pool_gmlp_mixer",
    )(zf, zf, zf, zf, pool_w, pool_scale, ln_g, ln_b, sgu_w, sgu_b_exp)


BISECT_CHECK_EVERY = 4
BISECT_MAX_ROUNDS = 10
ROW_BLOCK = 64


def _indexer_kernel(qi_ref, ka_ref, kb_ref, wi_ref, o_ref, score_ref, wb_ref, mid_ref, thr_ref, upper_ref,
                    jmid_ref, jstar_ref, *, top_k, seq):
    nkc, tq, kc = score_ref.shape
    i = pl.program_id(1)
    nk = ((i + 1) * tq + kc - 1) // kc
    kf = float(top_k)
    ncol = kc // LANES
    row_blocks = [slice(r * ROW_BLOCK, (r + 1) * ROW_BLOCK) for r in range(tq // ROW_BLOCK)]
    lane_ones = jnp.ones((LANES, LANES), BF16)
    lane_iota = lax.broadcasted_iota(jnp.int32, (1, LANES), 1)

    wts = wi_ref[:, 0:N_IDX_HEADS] * (N_IDX_HEADS ** -0.5 * D_IDX ** -0.5)
    for h in range(N_IDX_HEADS):
        wb_ref[h] = jnp.broadcast_to(wts[:, h:h + 1], (tq, LANES))

    row_pos = i * tq + lax.broadcasted_iota(jnp.int32, (tq, 1), 0)
    nt = (((1,), (1,)), ((), ()))

    def score_chunk(c, carry, masked):
        mn, mx = carry
        k0 = pl.multiple_of(c * kc, kc)
        ka = ka_ref[pl.ds(k0, kc), :]
        kb = kb_ref[pl.ds(k0, kc), :]
        cols = [jnp.zeros((tq, LANES), F32) for _ in range(ncol)]
        for p in range(N_IDX_HEADS // 2):
            qp = qi_ref[:, p * LANES:(p + 1) * LANES]
            for half, kk in enumerate((ka, kb)):
                lg = lax.dot_general(qp, kk, nt, preferred_element_type=F32)
                wbh = wb_ref[2 * p + half]
                for j in range(ncol):
                    cols[j] = cols[j] + jnp.maximum(lg[:, j * LANES:(j + 1) * LANES], 0.0) * wbh
        for j in range(ncol):
            sc = cols[j]
            if masked:
                causal = (k0 + j * LANES + lane_iota) <= row_pos
                score_ref[c, :, j * LANES:(j + 1) * LANES] = jnp.where(causal, sc, -jnp.inf)
                mx = jnp.maximum(mx, jnp.where(causal, sc, -jnp.inf))
                mn = jnp.minimum(mn, jnp.where(causal, sc, jnp.inf))
            else:
                score_ref[c, :, j * LANES:(j + 1) * LANES] = sc
                mx = jnp.maximum(mx, sc)
                mn = jnp.minimum(mn, sc)
        return mn, mx

    ext = (jnp.full((tq, LANES), jnp.inf, F32), jnp.full((tq, LANES), -jnp.inf, F32))
    ext = lax.fori_loop(0, nk - 1, functools.partial(score_chunk, masked=False), ext)
    mn, mx = score_chunk(nk - 1, ext, masked=True)
    lo = jnp.broadcast_to(jnp.min(mn, axis=-1, keepdims=True), (tq, LANES))
    hi = jnp.broadcast_to(jnp.max(mx, axis=-1, keepdims=True), (tq, LANES))

    def count(pred):
        def body(c, acc):
            parts = []
            for rows in row_blocks:
                a = acc[rows]
                for j in range(ncol):
                    s = score_ref[c, rows, j * LANES:(j + 1) * LANES]
                    a = a + jnp.where(pred(s, rows, c * kc + j * LANES), 1.0, 0.0)
                parts.append(a)
            return jnp.concatenate(parts, axis=0)

        acc = lax.fori_loop(0, nk, body, jnp.zeros((tq, LANES), F32))
        return jnp.dot(acc.astype(BF16), lane_ones, preferred_element_type=F32)

    def count_ge(ref):
        return count(lambda s, rows, col0: s >= ref[rows, :])

    mid_ref[...] = hi
    c_hi = count_ge(mid_ref)
    c_lo = jnp.broadcast_to((row_pos + 1).astype(F32), (tq, LANES))

    def bisect_step(_, st):
        lo, hi, c_lo, c_hi = st
        mid = 0.5 * lo + 0.5 * hi
        mid_ref[...] = mid
        cnt = count_ge(mid_ref)
        ge = cnt >= kf
        return jnp.where(ge, mid, lo), jnp.where(ge, hi, mid), jnp.where(ge, cnt, c_lo), jnp.where(ge, c_hi, cnt)

    def bisect_round(carry):
        r, st = carry
        return r + 1, lax.fori_loop(0, BISECT_CHECK_EVERY, bisect_step, st)

    def bisect_more(carry):
        r, (lo, hi, c_lo, c_hi) = carry
        return (r < BISECT_MAX_ROUNDS) & (jnp.max(c_lo) > kf)

    _, (lo, hi, c_lo, c_hi) = lax.while_loop(bisect_more, bisect_round, (0, (lo, hi, c_lo, c_hi)))

    at_max = c_hi >= kf
    thr_ref[...] = jnp.where(at_max, hi, lo)
    c_gt = jnp.where(at_max, 0.0, c_hi)
    c_thr = jnp.where(at_max, c_hi, c_lo)
    excess = c_thr > kf
    any_excess = jnp.max(jnp.where(excess, 1.0, 0.0)) > 0.0

    @pl.when(jnp.logical_not(any_excess))
    def _():
        def emit(c, _):
            for rows in row_blocks:
                t = thr_ref[rows, :]
                for j in range(ncol):
                    cs = slice(j * LANES, (j + 1) * LANES)
                    o_ref[c, rows, cs] = jnp.where(score_ref[c, rows, cs] >= t, 0.0, NEG_BIAS).astype(o_ref.dtype)
            return 0

        lax.fori_loop(0, nk, emit, 0)

    @pl.when(any_excess)
    def _():
        upper_ref[...] = jnp.where(at_max, jnp.inf, hi)
        need = kf - c_gt

        def in_class(s, rows):
            return (s >= thr_ref[rows, :]) & (s < upper_ref[rows, :])

        def jbisect(_, carry):
            jlo, jhi = carry
            jmid = (jlo + jhi) >> 1
            jmid_ref[...] = jmid
            cnt = count(lambda s, rows, col0: in_class(s, rows) & ((col0 + lane_iota) <= jmid_ref[rows, :]))
            ge = cnt >= need
            return jnp.where(ge, jlo, jmid), jnp.where(ge, jmid, jhi)

        _, jhi = lax.fori_loop(0, seq.bit_length(), jbisect,
                               (jnp.full((tq, LANES), -1, jnp.int32), jnp.full((tq, LANES), seq - 1, jnp.int32)))
        jstar_ref[...] = jnp.where(excess, jhi, seq)

        def emit(c, _):
            for rows in row_blocks:
                for j in range(ncol):
                    cs = slice(j * LANES, (j + 1) * LANES)
                    s = score_ref[c, rows, cs]
                    col = c * kc + j * LANES + lane_iota
                    sel = (s >= thr_ref[rows, :]) & ((s >= upper_ref[rows, :]) | (col <= jstar_ref[rows, :]))
                    o_ref[c, rows, cs] = jnp.where(sel, 0.0, NEG_BIAS).astype(o_ref.dtype)
            return 0

        lax.fori_loop(0, nk, emit, 0)

    def fill(c, _):
        o_ref[c] = jnp.full((tq, kc), NEG_BIAS, o_ref.dtype)
        return 0

    lax.fori_loop(nk, nkc, fill, 0)


def _indexer(zb, zf, batch, seq, tq, kc):
    nq = seq // tq
    nkc = seq // kc
    top_k = min(TOPK_MAX, seq // 4)
    rep = lambda dtype: pltpu.VMEM((tq, LANES), dtype)
    return pl.pallas_call(
        functools.partial(_indexer_kernel, top_k=top_k, seq=seq),
        grid=(batch, nq),
        in_specs=[
            pl.BlockSpec((tq, D_ATTN), lambda b, i: (b * nq + i, ZB_QI)),
            pl.BlockSpec((seq, LANES), lambda b, i: (b, ZB_KA)),
            pl.BlockSpec((seq, LANES), lambda b, i: (b, ZB_KB)),
            pl.BlockSpec((tq, LANES), lambda b, i: (b * nq + i, ZF_WI)),
        ],
        out_specs=pl.BlockSpec((None, nkc, tq, kc), lambda b, i: (b * nq + i, 0, 0, 0)),
        out_shape=jax.ShapeDtypeStruct((batch * nq, nkc, tq, kc), BF16),
        scratch_shapes=[
            pltpu.VMEM((nkc, tq, kc), F32),
            pltpu.VMEM((N_IDX_HEADS, tq, LANES), F32),
            rep(F32), rep(F32), rep(F32), rep(jnp.int32), rep(jnp.int32),
        ],
        compiler_params=_cparams(("parallel", "arbitrary")),
        name="dsa_indexer_topk",
    )(zb, zb, zb, zf)


ATTN_ROW_BLOCK = 64
LOG2E = 1.4426950408889634


def _attn_kernel(q_ref, k_ref, v_ref, b_ref, o_ref, m_ref, l_ref, acc_ref, s_ref, p_ref, bias_ref, alpha_ref):
    tq = q_ref.shape[0]
    kc = k_ref.shape[0]
    ncol = kc // LANES
    i = pl.program_id(1)
    c = pl.program_id(2)
    last = ((i + 1) * tq - 1) // kc
    nt = (((1,), (1,)), ((), ()))

    @pl.when(c == 0)
    def _():
        m_ref[...] = jnp.full(m_ref.shape, -jnp.inf, F32)
        l_ref[...] = jnp.zeros(l_ref.shape, F32)
        acc_ref[...] = jnp.zeros(acc_ref.shape, F32)

    @pl.when(c <= last)
    def _():
        bias_ref[...] = b_ref[...].astype(F32)

        def qk(h):
            cs = slice(h * HEAD_DIM, (h + 1) * HEAD_DIM)
            s_ref[h % 2] = lax.dot_general(q_ref[:, cs], k_ref[:, cs], nt, preferred_element_type=F32)

        qk(0)
        for h in range(N_ATTN_HEADS):
            cs = slice(h * HEAD_DIM, (h + 1) * HEAD_DIM)
            buf = h % 2
            if h + 1 < N_ATTN_HEADS:
                qk(h + 1)
            for r in range(tq // ATTN_ROW_BLOCK):
                rows = slice(r * ATTN_ROW_BLOCK, (r + 1) * ATTN_ROW_BLOCK)
                t = [s_ref[buf, rows, j * LANES:(j + 1) * LANES] * (HEAD_DIM ** -0.5 * LOG2E)
                     + bias_ref[rows, j * LANES:(j + 1) * LANES] for j in range(ncol)]
                tmax = t[0]
                for j in range(1, ncol):
                    tmax = jnp.maximum(tmax, t[j])
                m_prev = m_ref[h, rows, :]
                m_new = jnp.maximum(m_prev, jnp.max(tmax, axis=-1, keepdims=True))
                alpha = jnp.exp2(m_prev - m_new)
                p = [jnp.exp2(t[j] - m_new) for j in range(ncol)]
                psum = p[0]
                for j in range(1, ncol):
                    psum = psum + p[j]
                l_ref[h, rows, :] = alpha * l_ref[h, rows, :] + jnp.sum(psum, axis=-1, keepdims=True)
                m_ref[h, rows, :] = m_new
                alpha_ref[buf, rows, :] = alpha
                for j in range(ncol):
                    p_ref[buf, rows, j * LANES:(j + 1) * LANES] = p[j].astype(BF16)
            acc_ref[:, cs] = alpha_ref[buf] * acc_ref[:, cs] + jnp.dot(p_ref[buf], v_ref[:, cs],
                                                                     preferred_element_type=F32)

    @pl.when(c == pl.num_programs(2) - 1)
    def _():
        for h in range(N_ATTN_HEADS):
            cs = slice(h * HEAD_DIM, (h + 1) * HEAD_DIM)
            o_ref[:, cs] = (acc_ref[:, cs] / l_ref[h]).astype(o_ref.dtype)


def _attention(zb, bias, batch, seq, tq, kc):
    nq = seq // tq
    nkc = seq // kc

    def chunk(i, c):
        return jnp.minimum(c, ((i + 1) * tq - 1) // kc)

    return pl.pallas_call(
        _attn_kernel,
        grid=(batch, nq, nkc),
        in_specs=[
            pl.BlockSpec((tq, D_ATTN), lambda b, i, c: (b * nq + i, ZB_Q)),
            pl.BlockSpec((kc, D_ATTN), lambda b, i, c: (b * nkc + chunk(i, c), ZB_K)),
            pl.BlockSpec((kc, D_ATTN), lambda b, i, c: (b * nkc + chunk(i, c), ZB_V)),
            pl.BlockSpec((None, None, tq, kc), lambda b, i, c: (b * nq + i, chunk(i, c), 0, 0)),
        ],
        out_specs=pl.BlockSpec((tq, D_ATTN), lambda b, i, c: (b * nq + i, 0)),
        out_shape=jax.ShapeDtypeStruct((batch * seq, D_ATTN), BF16),
        scratch_shapes=[
            pltpu.VMEM((N_ATTN_HEADS, tq, LANES), F32),
            pltpu.VMEM((N_ATTN_HEADS, tq, LANES), F32),
            pltpu.VMEM((tq, D_ATTN), F32),
            pltpu.VMEM((2, tq, kc), F32),
            pltpu.VMEM((2, tq, kc), BF16),
            pltpu.VMEM((tq, kc), F32),
            pltpu.VMEM((2, tq, LANES), F32),
        ],
        compiler_params=_cparams(("parallel", "parallel", "arbitrary")),
        name="dsa_masked_attention",
    )(zb, zb, zb, bias)


def _out_proj_kernel(ypg_ref, ya_ref, w_ref, x_ref, g_ref, o_ref):
    mix = jnp.dot(ypg_ref[:, 0:D_POOL], w_ref[0:D_POOL, :], preferred_element_type=F32)
    mix = mix + jnp.dot(ya_ref[...], w_ref[D_POOL:D_POOL + D_ATTN, :], preferred_element_type=F32)
    mix = mix + jnp.dot(ypg_ref[:, D_POOL:], w_ref[D_POOL + D_ATTN:, :], preferred_element_type=F32)
    o_ref[...] = x_ref[...] + _rms(mix, g_ref[...])


def _out_proj(ypg, ya, w, layer, x, g, tm):
    m, d = x.shape
    return pl.pallas_call(
        _out_proj_kernel,
        grid=(m // tm,),
        in_specs=[
            pl.BlockSpec((tm, ypg.shape[1]), lambda i: (i, 0)),
            pl.BlockSpec((tm, ya.shape[1]), lambda i: (i, 0)),
            pl.BlockSpec((None,) + w.shape[1:], lambda i: (layer, 0, 0)),
            pl.BlockSpec((tm, d), lambda i: (i, 0)),
            pl.BlockSpec((1, d), lambda i: (0, 0)),
        ],
        out_specs=pl.BlockSpec((tm, d), lambda i: (i, 0)),
        out_shape=jax.ShapeDtypeStruct((m, d), F32),
        compiler_params=_cparams(("parallel",)),
        name="out_proj_residual",
    )(ypg, ya, w, x, g)


def _ffn_kernel(x_ref, xh_ref, g1_ref, wg_ref, wv_ref, cwg_ref, cwv_ref, cbg_ref, cbv_ref, wd_ref, g2_ref,
                o_ref, h_ref, up_ref, acc_ref, *, seq):
    tm = x_ref.shape[0]
    tf = wg_ref.shape[1]
    j = pl.program_id(1)
    at_start = (pl.program_id(0) * tm) % seq == 0

    @pl.when(j == 0)
    def _():
        h_ref[0:CONV_HALO, :] = _rms(xh_ref[...], g1_ref[...]).astype(BF16)
        h_ref[CONV_HALO:, :] = _rms(x_ref[...], g1_ref[...]).astype(BF16)
        acc_ref[...] = jnp.zeros(acc_ref.shape, F32)

    def conv(w_ref, cw_ref, cb_ref):
        up = jnp.dot(h_ref[...], w_ref[...], preferred_element_type=F32)
        up_ref[0:CONV_HALO, :] = jnp.where(at_start, 0.0, up[0:CONV_HALO, :])
        up_ref[CONV_HALO:, :] = up[CONV_HALO:, :]
        out = cb_ref[...] + cw_ref[CONV_W - 1:CONV_W, :] * up[CONV_HALO:, :]
        for t in range(CONV_W - 1):
            shift = CONV_W - 1 - t
            out = out + cw_ref[t:t + 1, :] * up_ref[pl.ds(CONV_HALO - shift, tm), :]
        return out

    gate = conv(wg_ref, cwg_ref, cbg_ref)
    act = gate * (1.0 / (1.0 + jnp.exp(-gate)))
    val = conv(wv_ref, cwv_ref, cbv_ref)
    acc_ref[...] += jnp.dot((act * val).astype(BF16), wd_ref[...], preferred_element_type=F32)

    @pl.when(j == pl.num_programs(1) - 1)
    def _():
        o_ref[...] = x_ref[...] + _rms(acc_ref[...], g2_ref[...])


def _conv_ffn(x, g1, w_up, conv_w, conv_b, w_down, layer, g2, seq, tm, tf):
    m, d = x.shape
    nf = D_FF // tf
    hb = tm // CONV_HALO
    return pl.pallas_call(
        functools.partial(_ffn_kernel, seq=seq),
        grid=(m // tm, nf),
        in_specs=[
            pl.BlockSpec((tm, d), lambda i, j: (i, 0)),
            pl.BlockSpec((CONV_HALO, d), lambda i, j: (jnp.maximum(i * hb - 1, 0), 0)),
            pl.BlockSpec((1, d), lambda i, j: (0, 0)),
            pl.BlockSpec((None, d, tf), lambda i, j: (layer, 0, j)),
            pl.BlockSpec((None, d, tf), lambda i, j: (layer, 0, nf + j)),
            pl.BlockSpec((CONV_W, tf), lambda i, j: (0, j)),
            pl.BlockSpec((CONV_W, tf), lambda i, j: (0, nf + j)),
            pl.BlockSpec((1, tf), lambda i, j: (0, j)),
            pl.BlockSpec((1, tf), lambda i, j: (0, nf + j)),
            pl.BlockSpec((None, tf, d), lambda i, j: (layer, j, 0)),
            pl.BlockSpec((1, d), lambda i, j: (0, 0)),
        ],
        out_specs=pl.BlockSpec((tm, d), lambda i, j: (i, 0)),
        out_shape=jax.ShapeDtypeStruct((m, d), F32),
        scratch_shapes=[
            pltpu.VMEM((CONV_HALO + tm, d), BF16),
            pltpu.VMEM((CONV_HALO + tm, tf), F32),
            pltpu.VMEM((tm, d), F32),
        ],
        compiler_params=_cparams(("parallel", "arbitrary")),
        name="conv_ffn_residual",
    )(x, x, g1, w_up, w_up, conv_w, conv_w, conv_b, conv_b, w_down, g2)


def _split_w_in(w):
    o_q = D_POOL
    o_ki = o_q + 3 * D_ATTN + N_IDX_HEADS * D_IDX
    o_wi = o_ki + D_IDX
    o_g = o_wi + N_IDX_HEADS
    ki = w[:, :, o_ki:o_wi]
    zk = jnp.zeros_like(ki)
    wb = jnp.concatenate([w[:, :, o_q:o_wi], zk, zk, ki], axis=2)
    zw = jnp.zeros(w.shape[:2] + (LANES - N_IDX_HEADS,), w.dtype)
    wf = jnp.concatenate([w[:, :, 0:o_q], w[:, :, o_g:], w[:, :, o_wi:o_g], zw], axis=2)
    return wb.astype(BF16), wf.astype(BF16)


def _trunk(x, w_in, pool_w, pool_scale, sgu_ln_g, sgu_ln_b, sgu_w, sgu_b, w_out, ffn_w_up, ffn_conv_w,
           ffn_conv_b, ffn_w_down, norm_pre_mix, norm_post_mix, norm_pre_ffn, norm_post_ffn,
           *, tm, tq, kc, tf):
    batch, seq, d = x.shape
    depth = w_in.shape[0]
    xf = x.reshape(batch * seq, d)
    row = lambda v: v.reshape(1, -1)
    wb, wf = _split_w_in(w_in)
    wo = w_out.astype(BF16)
    w_up = ffn_w_up.astype(BF16)
    w_down = ffn_w_down.astype(BF16)
    pw = pool_w.astype(BF16)
    for l in range(depth):
        g = row(norm_pre_mix[l])
        zb = _norm_matmul(xf, g, wb, l, BF16, tm, ZB_WIDTH // 2)
        zf = _norm_matmul(xf, g, wf, l, F32, tm, ZF_WIDTH)
        sb_exp = jnp.broadcast_to(sgu_b[l][:, :, None], (GMLP_HEADS, CHUNK, HEAD_DIM))
        ypg = _mixer(zf, pw[l], row(pool_scale[l]), row(sgu_ln_g[l]), row(sgu_ln_b[l]), sgu_w[l], sb_exp, seq, tm)
        bias = _indexer(zb, zf, batch, seq, tq, kc)
        ya = _attention(zb, bias, batch, seq, tq, kc)
        xf = _out_proj(ypg, ya, wo, l, xf, row(norm_post_mix[l]), tm)
        xf = _conv_ffn(xf, row(norm_pre_ffn[l]), w_up, ffn_conv_w[l], row(ffn_conv_b[l]), w_down, l,
                       row(norm_post_ffn[l]), seq, tm, tf)
    return xf.reshape(batch, seq, d)


def kernel(x, w_in, pool_w, pool_scale, sgu_ln_g, sgu_ln_b, sgu_w, sgu_b, w_out, ffn_w_up, ffn_conv_w,
           ffn_conv_b, ffn_w_down, norm_pre_mix, norm_post_mix, norm_pre_ffn, norm_post_ffn):
    return _trunk(x, w_in, pool_w, pool_scale, sgu_ln_g, sgu_ln_b, sgu_w, sgu_b, w_out, ffn_w_up, ffn_conv_w,
                  ffn_conv_b, ffn_w_down, norm_pre_mix, norm_post_mix, norm_pre_ffn, norm_post_ffn,
                  tm=512, tq=256, kc=512, tf=512)
```

```python
import functools

import jax
import jax.numpy as jnp
from jax import lax
from jax.experimental import pallas as pl
from jax.experimental.pallas import tpu as pltpu

F32 = jnp.float32
BF16 = jnp.bfloat16

D_MODEL = 2048
HEAD_DIM = 128
POOL_WINDOWS = (2, 4, 8, 16)
D_POOL = len(POOL_WINDOWS) * HEAD_DIM
POOL_HALO = 16
N_ATTN_HEADS = 8
D_ATTN = N_ATTN_HEADS * HEAD_DIM
N_IDX_HEADS = 16
D_IDX = 64
TOPK_MAX = 256
GMLP_HEADS = 4
D_GMLP = GMLP_HEADS * HEAD_DIM
CHUNK = 128
D_FF = 5632
CONV_W = 3
CONV_HALO = 8
EPS = 1e-6

LANES = 128
NEG_BIAS = -1e30

ZB_WIDTH = 4 * D_ATTN + 2 * LANES
ZB_Q, ZB_K, ZB_V, ZB_QI = 0, 1, 2, 3
ZB_KA = 4 * D_ATTN // LANES
ZB_KB = ZB_KA + 1
ZF_WIDTH = D_POOL + 2 * D_GMLP + LANES
ZF_WI = (D_POOL + 2 * D_GMLP) // LANES

VMEM_LIMIT = 56 * 1024 * 1024


def _cparams(sem):
    return pltpu.CompilerParams(dimension_semantics=sem, vmem_limit_bytes=VMEM_LIMIT)


def _rms(x, g):
    ms = jnp.mean(x * x, axis=-1, keepdims=True)
    return x * lax.rsqrt(ms + EPS) * g


def _norm_matmul_kernel(x_ref, g_ref, w_ref, o_ref, h_ref):
    @pl.when(pl.program_id(1) == 0)
    def _():
        h_ref[...] = _rms(x_ref[...], g_ref[...]).astype(BF16)

    o_ref[...] = jnp.dot(h_ref[...], w_ref[...], preferred_element_type=F32).astype(o_ref.dtype)


def _norm_matmul(x, g, w, layer, out_dtype, tm, tn):
    m, d = x.shape
    n = w.shape[2]
    return pl.pallas_call(
        _norm_matmul_kernel,
        grid=(m // tm, n // tn),
        in_specs=[
            pl.BlockSpec((tm, d), lambda i, j: (i, 0)),
            pl.BlockSpec((1, d), lambda i, j: (0, 0)),
            pl.BlockSpec((None, d, tn), lambda i, j: (layer, 0, j)),
        ],
        out_specs=pl.BlockSpec((tm, tn), lambda i, j: (i, j)),
        out_shape=jax.ShapeDtypeStruct((m, n), out_dtype),
        scratch_shapes=[pltpu.VMEM((tm, d), BF16)],
        compiler_params=_cparams(("parallel", "arbitrary")),
        name="norm_in_proj",
    )(x, g, w)


def _gelu_tanh(x):
    return 0.5 * x * (1.0 + jnp.tanh(0.7978845608028654 * (x + 0.044715 * (x * x * x))))


def _mixer_kernel(a_ref, halo_ref, u_ref, v_ref, pw_ref, ps_ref, lg_ref, lb_ref, sw_ref, sb_ref,
                  o_ref, ext_ref, *, seq):
    tt = a_ref.shape[0]
    t0 = (pl.program_id(0) * tt) % seq
    a = a_ref[...]
    ext_ref[0:POOL_HALO, :] = jnp.where(t0 == 0, 0.0, halo_ref[...])
    ext_ref[POOL_HALO:, :] = a
    pos = (t0 + 1 + lax.broadcasted_iota(jnp.int32, (tt, 1), 0)).astype(F32)
    for g, w in enumerate(POOL_WINDOWS):
        cs = slice(g * HEAD_DIM, (g + 1) * HEAD_DIM)
        ag = a[:, cs]
        acc = ag
        for j in range(1, w):
            acc = acc + ext_ref[pl.ds(POOL_HALO - j, tt), cs]
        pooled = acc / jnp.minimum(pos, float(w)) - ag
        y = jnp.dot(pooled.astype(BF16), pw_ref[g], preferred_element_type=F32) * ps_ref[:, cs]
        o_ref[:, cs] = y.astype(o_ref.dtype)

    zu = _gelu_tanh(u_ref[...])
    zv = _gelu_tanh(v_ref[...])
    mu = jnp.mean(zv, axis=-1, keepdims=True)
    dv = zv - mu
    var = jnp.mean(dv * dv, axis=-1, keepdims=True)
    vn = (dv * lax.rsqrt(var + EPS) * lg_ref[...] + lb_ref[...]).astype(BF16)
    tril = (lax.broadcasted_iota(jnp.int32, (CHUNK, CHUNK), 0)
            >= lax.broadcasted_iota(jnp.int32, (CHUNK, CHUNK), 1))
    for g in range(GMLP_HEADS):
        cs = slice(g * HEAD_DIM, (g + 1) * HEAD_DIM)
        wg = jnp.where(tril, sw_ref[g], 0.0).astype(BF16)
        for c in range(tt // CHUNK):
            rs = slice(c * CHUNK, (c + 1) * CHUNK)
            mixed = jnp.dot(wg, vn[rs, cs], preferred_element_type=F32) + sb_ref[g]
            o_ref[rs, D_POOL + g * HEAD_DIM:D_POOL + (g + 1) * HEAD_DIM] = (zu[rs, cs] * mixed).astype(o_ref.dtype)


def _mixer(zf, pool_w, pool_scale, ln_g, ln_b, sgu_w, sgu_b_exp, seq, tt):
    m = zf.shape[0]
    hb = tt // POOL_HALO
    const3 = lambda i: (0, 0, 0)
    const2 = lambda i: (0, 0)
    return pl.pallas_call(
        functools.partial(_mixer_kernel, seq=seq),
        grid=(m // tt,),
        in_specs=[
            pl.BlockSpec((tt, D_POOL), lambda i: (i, 0)),
            pl.BlockSpec((POOL_HALO, D_POOL), lambda i: (jnp.maximum(i * hb - 1, 0), 0)),
            pl.BlockSpec((tt, D_GMLP), lambda i: (i, D_POOL // D_GMLP)),
            pl.BlockSpec((tt, D_GMLP), lambda i: (i, D_POOL // D_GMLP + 1)),
            pl.BlockSpec((len(POOL_WINDOWS), HEAD_DIM, HEAD_DIM), const3),
            pl.BlockSpec((1, D_POOL), const2),
            pl.BlockSpec((1, D_GMLP), const2),
            pl.BlockSpec((1, D_GMLP), const2),
            pl.BlockSpec((GMLP_HEADS, CHUNK, CHUNK), const3),
            pl.BlockSpec((GMLP_HEADS, CHUNK, HEAD_DIM), const3),
        ],
        out_specs=pl.BlockSpec((tt, D_POOL + D_GMLP), lambda i: (i, 0)),
        out_shape=jax.ShapeDtypeStruct((m, D_POOL + D_GMLP), BF16),
        scratch_shapes=[pltpu.VMEM((POOL_HALO + tt, D_POOL), F32)],
        compiler_params=_cparams(("parallel",)),
        name="pool_gmlp_mixer",
    )(zf, zf, zf, zf, pool_w, pool_scale, ln_g, ln_b, sgu_w, sgu_b_exp)


BISECT_CHECK_EVERY = 4
BISECT_MAX_ROUNDS = 10
SUBLANES = 8
N_PARTIAL = 4


def _fold_rows(x, op):
    rows, n = x.shape
    y = x.reshape(rows // (N_PARTIAL * SUBLANES), N_PARTIAL, SUBLANES, n)
    return op(op(y, axis=0), axis=0)


def _indexer_kernel(qi_ref, ka_ref, kb_ref, wi_ref, o_ref, score_ref, *, top_k, seq):
    nkc, kc, tq = score_ref.shape
    i = pl.program_id(1)
    nk = ((i + 1) * tq + kc - 1) // kc
    kf = float(top_k)
    nt = (((1,), (1,)), ((), ()))

    wts = (wi_ref[...] * (N_IDX_HEADS ** -0.5 * D_IDX ** -0.5)).T
    qpos = i * tq + lax.broadcasted_iota(jnp.int32, (1, tq), 1)

    def key_index(c, r=0, rows=None):
        return c * kc + r + lax.broadcasted_iota(jnp.int32, (kc if rows is None else rows, tq), 0)

    def score_chunk(c, carry, masked):
        mn, mx = carry
        k0 = pl.multiple_of(c * kc, kc)
        ka = ka_ref[pl.ds(k0, kc), :]
        kb = kb_ref[pl.ds(k0, kc), :]
        sc = jnp.zeros((kc, tq), F32)
        for p in range(N_IDX_HEADS // 2):
            qp = qi_ref[:, p * LANES:(p + 1) * LANES]
            for half, kk in enumerate((ka, kb)):
                h = 2 * p + half
                lg = lax.dot_general(kk, qp, nt, preferred_element_type=F32)
                sc = sc + jnp.maximum(lg, 0.0) * wts[h:h + 1, :]
        if masked:
            causal = key_index(c) <= qpos
            sc_hi = jnp.where(causal, sc, jnp.inf)
            sc = jnp.where(causal, sc, -jnp.inf)
        else:
            sc_hi = sc
        score_ref[c] = sc
        return jnp.minimum(mn, _fold_rows(sc_hi, jnp.min)), jnp.maximum(mx, _fold_rows(sc, jnp.max))

    ext = (jnp.full((SUBLANES, tq), jnp.inf, F32), jnp.full((SUBLANES, tq), -jnp.inf, F32))
    ext = lax.fori_loop(0, nk - 1, functools.partial(score_chunk, masked=False), ext)
    mn, mx = score_chunk(nk - 1, ext, masked=True)
    lo = jnp.min(mn, axis=0, keepdims=True)
    hi = jnp.max(mx, axis=0, keepdims=True)

    def count(pred):
        def body(c, accs):
            accs = list(accs)
            for r in range(0, kc, SUBLANES):
                hit = pred(score_ref[c, r:r + SUBLANES, :], c, r)
                accs[(r // SUBLANES) % N_PARTIAL] += jnp.where(hit, 1.0, 0.0)
            return tuple(accs)

        accs = lax.fori_loop(0, nk, body, tuple(jnp.zeros((SUBLANES, tq), F32) for _ in range(N_PARTIAL)))
        return jnp.sum(functools.reduce(lambda a, b: a + b, accs), axis=0, keepdims=True)

    c_hi = count(lambda s, c, r: s >= hi)
    c_lo = (qpos + 1).astype(F32)

    def bisect_step(_, st):
        lo, hi, c_lo, c_hi = st
        mid = 0.5 * lo + 0.5 * hi
        cnt = count(lambda s, c, r: s >= mid)
        ge = cnt >= kf
        return jnp.where(ge, mid, lo), jnp.where(ge, hi, mid), jnp.where(ge, cnt, c_lo), jnp.where(ge, c_hi, cnt)

    def bisect_round(carry):
        r, st = carry
        return r + 1, lax.fori_loop(0, BISECT_CHECK_EVERY, bisect_step, st)

    def bisect_more(carry):
        r, (lo, hi, c_lo, c_hi) = carry
        return (r < BISECT_MAX_ROUNDS) & (jnp.max(c_lo) > kf)

    _, (lo, hi, c_lo, c_hi) = lax.while_loop(bisect_more, bisect_round, (0, (lo, hi, c_lo, c_hi)))

    at_max = c_hi >= kf
    thr = jnp.where(at_max, hi, lo)
    c_gt = jnp.where(at_max, 0.0, c_hi)
    c_thr = jnp.where(at_max, c_hi, c_lo)
    excess = c_thr > kf
    any_excess = jnp.max(jnp.where(excess, 1.0, 0.0)) > 0.0
    eye = (lax.broadcasted_iota(jnp.int32, (tq, tq), 0) == lax.broadcasted_iota(jnp.int32, (tq, tq), 1)).astype(BF16)

    def emit(c, sel):
        mask_t = jnp.where(sel, 0.0, NEG_BIAS).astype(BF16)
        o_ref[c] = lax.dot_general(eye, mask_t, nt, preferred_element_type=F32).astype(o_ref.dtype)

    @pl.when(jnp.logical_not(any_excess))
    def _():
        def body(c, _):
            emit(c, score_ref[c] >= thr)
            return 0

        lax.fori_loop(0, nk, body, 0)

    @pl.when(any_excess)
    def _():
        upper = jnp.where(at_max, jnp.inf, hi)
        need = kf - c_gt

        def jbisect(_, carry):
            jlo, jhi = carry
            jmid = (jlo + jhi) >> 1
            cnt = count(lambda s, c, r: (s >= thr) & (s < upper) & (key_index(c, r, SUBLANES) <= jmid))
            ge = cnt >= need
            return jnp.where(ge, jlo, jmid), jnp.where(ge, jmid, jhi)

        _, jhi = lax.fori_loop(0, seq.bit_length(), jbisect,
                               (jnp.full((1, tq), -1, jnp.int32), jnp.full((1, tq), seq - 1, jnp.int32)))
        jstar = jnp.where(excess, jhi, seq)

        def body(c, _):
            s = score_ref[c]
            emit(c, (s >= thr) & ((s >= upper) | (key_index(c) <= jstar)))
            return 0

        lax.fori_loop(0, nk, body, 0)

    def fill(c, _):
        o_ref[c] = jnp.full((tq, kc), NEG_BIAS, o_ref.dtype)
        return 0

    lax.fori_loop(nk, nkc, fill, 0)


def _indexer(zb, zf, batch, seq, tq, kc):
    nq = seq // tq
    nkc = seq // kc
    top_k = min(TOPK_MAX, seq // 4)
    return pl.pallas_call(
        functools.partial(_indexer_kernel, top_k=top_k, seq=seq),
        grid=(batch, nq),
        in_specs=[
            pl.BlockSpec((tq, D_ATTN), lambda b, i: (b * nq + i, ZB_QI)),
            pl.BlockSpec((seq, LANES), lambda b, i: (b, ZB_KA)),
            pl.BlockSpec((seq, LANES), lambda b, i: (b, ZB_KB)),
            pl.BlockSpec((tq, LANES), lambda b, i: (b * nq + i, ZF_WI)),
        ],
        out_specs=pl.BlockSpec((None, nkc, tq, kc), lambda b, i: (b * nq + i, 0, 0, 0)),
        out_shape=jax.ShapeDtypeStruct((batch * nq, nkc, tq, kc), BF16),
        scratch_shapes=[pltpu.VMEM((nkc, kc, tq), F32)],
        compiler_params=_cparams(("parallel", "arbitrary")),
        name="dsa_indexer_topk",
    )(zb, zb, zb, zf)


ATTN_ROW_BLOCK = 64
LOG2E = 1.4426950408889634


def _attn_kernel(qt_ref, ct_ref, lt_ref, q_ref, k_ref, v_ref, b_ref, o_ref,
                 m_ref, l_ref, acc_ref, s_ref, p_ref, bias_ref, alpha_ref):
    tq = q_ref.shape[0]
    kc = k_ref.shape[0]
    ncol = kc // LANES
    step = pl.program_id(1)
    nt = (((1,), (1,)), ((), ()))

    @pl.when(ct_ref[step] == 0)
    def _():
        m_ref[...] = jnp.full(m_ref.shape, -jnp.inf, F32)
        l_ref[...] = jnp.zeros(l_ref.shape, F32)
        acc_ref[...] = jnp.zeros(acc_ref.shape, F32)

    bias_ref[...] = b_ref[...].astype(F32)

    def qk(h):
        cs = slice(h * HEAD_DIM, (h + 1) * HEAD_DIM)
        s_ref[h % 2] = lax.dot_general(q_ref[:, cs], k_ref[:, cs], nt, preferred_element_type=F32)

    qk(0)
    for h in range(N_ATTN_HEADS):
        cs = slice(h * HEAD_DIM, (h + 1) * HEAD_DIM)
        buf = h % 2
        if h + 1 < N_ATTN_HEADS:
            qk(h + 1)
        for r in range(tq // ATTN_ROW_BLOCK):
            rows = slice(r * ATTN_ROW_BLOCK, (r + 1) * ATTN_ROW_BLOCK)
            t = [s_ref[buf, rows, j * LANES:(j + 1) * LANES] * (HEAD_DIM ** -0.5 * LOG2E)
                 + bias_ref[rows, j * LANES:(j + 1) * LANES] for j in range(ncol)]
            tmax = t[0]
            for j in range(1, ncol):
                tmax = jnp.maximum(tmax, t[j])
            m_prev = m_ref[h, rows, :]
            m_new = jnp.maximum(m_prev, jnp.max(tmax, axis=-1, keepdims=True))
            alpha = jnp.exp2(m_prev - m_new)
            p = [jnp.exp2(t[j] - m_new) for j in range(ncol)]
            psum = p[0]
            for j in range(1, ncol):
                psum = psum + p[j]
            l_ref[h, rows, :] = alpha * l_ref[h, rows, :] + jnp.sum(psum, axis=-1, keepdims=True)
            m_ref[h, rows, :] = m_new
            alpha_ref[buf, rows, :] = alpha
            for j in range(ncol):
                p_ref[buf, rows, j * LANES:(j + 1) * LANES] = p[j].astype(BF16)
        acc_ref[:, cs] = alpha_ref[buf] * acc_ref[:, cs] + jnp.dot(p_ref[buf], v_ref[:, cs],
                                                                 preferred_element_type=F32)

    @pl.when(lt_ref[step] == 1)
    def _():
        for h in range(N_ATTN_HEADS):
            cs = slice(h * HEAD_DIM, (h + 1) * HEAD_DIM)
            o_ref[:, cs] = (acc_ref[:, cs] / l_ref[h]).astype(o_ref.dtype)


def _attention(zb, bias, batch, seq, tq, kc):
    nq = seq // tq
    nkc = seq // kc
    pairs = [(i, c) for i in range(nq) for c in range(((i + 1) * tq - 1) // kc + 1)]
    q_tab = jnp.asarray([i for i, _ in pairs], jnp.int32)
    c_tab = jnp.asarray([c for _, c in pairs], jnp.int32)
    l_tab = jnp.asarray([int(n + 1 == len(pairs) or pairs[n + 1][0] != i) for n, (i, _) in enumerate(pairs)],
                        jnp.int32)
    grid_spec = pltpu.PrefetchScalarGridSpec(
        num_scalar_prefetch=3,
        grid=(batch, len(pairs)),
        in_specs=[
            pl.BlockSpec((tq, D_ATTN), lambda b, t, qt, ct, lt: (b * nq + qt[t], ZB_Q)),
            pl.BlockSpec((kc, D_ATTN), lambda b, t, qt, ct, lt: (b * nkc + ct[t], ZB_K)),
            pl.BlockSpec((kc, D_ATTN), lambda b, t, qt, ct, lt: (b * nkc + ct[t], ZB_V)),
            pl.BlockSpec((None, None, tq, kc), lambda b, t, qt, ct, lt: (b * nq + qt[t], ct[t], 0, 0)),
        ],
        out_specs=pl.BlockSpec((tq, D_ATTN), lambda b, t, qt, ct, lt: (b * nq + qt[t], 0)),
        scratch_shapes=[
            pltpu.VMEM((N_ATTN_HEADS, tq, LANES), F32),
            pltpu.VMEM((N_ATTN_HEADS, tq, LANES), F32),
            pltpu.VMEM((tq, D_ATTN), F32),
            pltpu.VMEM((2, tq, kc), F32),
            pltpu.VMEM((2, tq, kc), BF16),
            pltpu.VMEM((tq, kc), F32),
            pltpu.VMEM((2, tq, LANES), F32),
        ],
    )
    return pl.pallas_call(
        _attn_kernel,
        grid_spec=grid_spec,
        out_shape=jax.ShapeDtypeStruct((batch * seq, D_ATTN), BF16),
        compiler_params=_cparams(("parallel", "arbitrary")),
        name="dsa_masked_attention",
    )(q_tab, c_tab, l_tab, zb, zb, zb, bias)


def _out_proj_kernel(ypg_ref, ya_ref, w_ref, x_ref, g_ref, o_ref):
    mix = jnp.dot(ypg_ref[:, 0:D_POOL], w_ref[0:D_POOL, :], preferred_element_type=F32)
    mix = mix + jnp.dot(ya_ref[...], w_ref[D_POOL:D_POOL + D_ATTN, :], preferred_element_type=F32)
    mix = mix + jnp.dot(ypg_ref[:, D_POOL:], w_ref[D_POOL + D_ATTN:, :], preferred_element_type=F32)
    o_ref[...] = x_ref[...] + _rms(mix, g_ref[...])


def _out_proj(ypg, ya, w, layer, x, g, tm):
    m, d = x.shape
    return pl.pallas_call(
        _out_proj_kernel,
        grid=(m // tm,),
        in_specs=[
            pl.BlockSpec((tm, ypg.shape[1]), lambda i: (i, 0)),
            pl.BlockSpec((tm, ya.shape[1]), lambda i: (i, 0)),
            pl.BlockSpec((None,) + w.shape[1:], lambda i: (layer, 0, 0)),
            pl.BlockSpec((tm, d), lambda i: (i, 0)),
            pl.BlockSpec((1, d), lambda i: (0, 0)),
        ],
        out_specs=pl.BlockSpec((tm, d), lambda i: (i, 0)),
        out_shape=jax.ShapeDtypeStruct((m, d), F32),
        compiler_params=_cparams(("parallel",)),
        name="out_proj_residual",
    )(ypg, ya, w, x, g)


def _ffn_kernel(x_ref, xh_ref, g1_ref, wg_ref, wv_ref, cwg_ref, cwv_ref, cbg_ref, cbv_ref, wd_ref, g2_ref,
                o_ref, h_ref, up_ref, acc_ref, *, seq):
    tm = x_ref.shape[0]
    j = pl.program_id(1)
    at_start = (pl.program_id(0) * tm) % seq == 0

    @pl.when(j == 0)
    def _():
        h_ref[0:CONV_HALO, :] = _rms(xh_ref[...], g1_ref[...]).astype(BF16)
        h_ref[CONV_HALO:, :] = _rms(x_ref[...], g1_ref[...]).astype(BF16)
        acc_ref[...] = jnp.zeros(acc_ref.shape, F32)

    def conv(w_ref, cw_ref, cb_ref):
        up = jnp.dot(h_ref[...], w_ref[...], preferred_element_type=F32)
        up_ref[0:CONV_HALO, :] = jnp.where(at_start, 0.0, up[0:CONV_HALO, :])
        up_ref[CONV_HALO:, :] = up[CONV_HALO:, :]
        out = cb_ref[...] + cw_ref[CONV_W - 1:CONV_W, :] * up[CONV_HALO:, :]
        for t in range(CONV_W - 1):
            shift = CONV_W - 1 - t
            out = out + cw_ref[t:t + 1, :] * up_ref[pl.ds(CONV_HALO - shift, tm), :]
        return out

    gate = conv(wg_ref, cwg_ref, cbg_ref)
    act = gate * (1.0 / (1.0 + jnp.exp(-gate)))
    val = conv(wv_ref, cwv_ref, cbv_ref)
    acc_ref[...] += jnp.dot((act * val).astype(BF16), wd_ref[...], preferred_element_type=F32)

    @pl.when(j == pl.num_programs(1) - 1)
    def _():
        o_ref[...] = x_ref[...] + _rms(acc_ref[...], g2_ref[...])


def _conv_ffn(x, g1, w_up, conv_w, conv_b, w_down, layer, g2, seq, tm, tf):
    m, d = x.shape
    nf = D_FF // tf
    hb = tm // CONV_HALO
    return pl.pallas_call(
        functools.partial(_ffn_kernel, seq=seq),
        grid=(m // tm, nf),
        in_specs=[
            pl.BlockSpec((tm, d), lambda i, j: (i, 0)),
            pl.BlockSpec((CONV_HALO, d), lambda i, j: (jnp.maximum(i * hb - 1, 0), 0)),
            pl.BlockSpec((1, d), lambda i, j: (0, 0)),
            pl.BlockSpec((None, d, tf), lambda i, j: (layer, 0, j)),
            pl.BlockSpec((None, d, tf), lambda i, j: (layer, 0, nf + j)),
            pl.BlockSpec((CONV_W, tf), lambda i, j: (0, j)),
            pl.BlockSpec((CONV_W, tf), lambda i, j: (0, nf + j)),
            pl.BlockSpec((1, tf), lambda i, j: (0, j)),
            pl.BlockSpec((1, tf), lambda i, j: (0, nf + j)),
            pl.BlockSpec((None, tf, d), lambda i, j: (layer, j, 0)),
            pl.BlockSpec((1, d), lambda i, j: (0, 0)),
        ],
        out_specs=pl.BlockSpec((tm, d), lambda i, j: (i, 0)),
        out_shape=jax.ShapeDtypeStruct((m, d), F32),
        scratch_shapes=[
            pltpu.VMEM((CONV_HALO + tm, d), BF16),
            pltpu.VMEM((CONV_HALO + tm, tf), F32),
            pltpu.VMEM((tm, d), F32),
        ],
        compiler_params=_cparams(("parallel", "arbitrary")),
        name="conv_ffn_residual",
    )(x, x, g1, w_up, w_up, conv_w, conv_w, conv_b, conv_b, w_down, g2)


def _split_w_in(w):
    o_q = D_POOL
    o_ki = o_q + 3 * D_ATTN + N_IDX_HEADS * D_IDX
    o_wi = o_ki + D_IDX
    o_g = o_wi + N_IDX_HEADS
    w = w.astype(BF16)
    ki = w[:, :, o_ki:o_wi]
    zk = jnp.zeros_like(ki)
    wb = jnp.concatenate([w[:, :, o_q:o_wi], zk, zk, ki], axis=2)
    zw = jnp.zeros(w.shape[:2] + (LANES - N_IDX_HEADS,), w.dtype)
    wf = jnp.concatenate([w[:, :, 0:o_q], w[:, :, o_g:], w[:, :, o_wi:o_g], zw], axis=2)
    return wb, wf


def _trunk(x, w_in, pool_w, pool_scale, sgu_ln_g, sgu_ln_b, sgu_w, sgu_b, w_out, ffn_w_up, ffn_conv_w,
           ffn_conv_b, ffn_w_down, norm_pre_mix, norm_post_mix, norm_pre_ffn, norm_post_ffn,
           *, tm, tq, kc, tf):
    batch, seq, d = x.shape
    depth = w_in.shape[0]
    xf = x.reshape(batch * seq, d)
    row = lambda v: v.reshape(1, -1)
    wb, wf = _split_w_in(w_in)
    wo = w_out.astype(BF16)
    w_up = ffn_w_up.astype(BF16)
    w_down = ffn_w_down.astype(BF16)
    pw = pool_w.astype(BF16)
    for l in range(depth):
        g = row(norm_pre_mix[l])
        zb = _norm_matmul(xf, g, wb, l, BF16, tm, ZB_WIDTH // 2)
        zf = _norm_matmul(xf, g, wf, l, F32, tm, ZF_WIDTH)
        sb_exp = jnp.broadcast_to(sgu_b[l][:, :, None], (GMLP_HEADS, CHUNK, HEAD_DIM))
        ypg = _mixer(zf, pw[l], row(pool_scale[l]), row(sgu_ln_g[l]), row(sgu_ln_b[l]), sgu_w[l], sb_exp, seq, tm)
        bias = _indexer(zb, zf, batch, seq, tq, kc)
        ya = _attention(zb, bias, batch, seq, tq, kc)
        xf = _out_proj(ypg, ya, wo, l, xf, row(norm_post_mix[l]), tm)
        xf = _conv_ffn(xf, row(norm_pre_ffn[l]), w_up, ffn_conv_w[l], row(ffn_conv_b[l]), w_down, l,
                       row(norm_post_ffn[l]), seq, tm, tf)
    return xf.reshape(batch, seq, d)


def kernel(x, w_in, pool_w, pool_scale, sgu_ln_g, sgu_ln_b, sgu_w, sgu_b, w_out, ffn_w_up, ffn_conv_w,
           ffn_conv_b, ffn_w_down, norm_pre_mix, norm_post_mix, norm_pre_ffn, norm_post_ffn):
    return _trunk(x, w_in, pool_w, pool_scale, sgu_ln_g, sgu_ln_b, sgu_w, sgu_b, w_out, ffn_w_up, ffn_conv_w,
                  ffn_conv_b, ffn_w_down, norm_pre_mix, norm_post_mix, norm_pre_ffn, norm_post_ffn,
                  tm=512, tq=256, kc=512, tf=512)
```

```python
import functools

import jax
import jax.numpy as jnp
from jax import lax
from jax.experimental import pallas as pl
from jax.experimental.pallas import tpu as pltpu

F32 = jnp.float32
BF16 = jnp.bfloat16

D_MODEL = 2048
HEAD_DIM = 128
POOL_WINDOWS = (2, 4, 8, 16)
D_POOL = len(POOL_WINDOWS) * HEAD_DIM
POOL_HALO = 16
N_ATTN_HEADS = 8
D_ATTN = N_ATTN_HEADS * HEAD_DIM
N_IDX_HEADS = 16
D_IDX = 64
TOPK_MAX = 256
GMLP_HEADS = 4
D_GMLP = GMLP_HEADS * HEAD_DIM
CHUNK = 128
D_FF = 5632
CONV_W = 3
CONV_HALO = 8
EPS = 1e-6

LANES = 128
NEG_BIAS = -1e30

ZB_WIDTH = 4 * D_ATTN
ZB_Q, ZB_K, ZB_V, ZB_QI = 0, 1, 2, 3
ZB_TN = D_POOL
ZF_WIDTH = D_POOL + 2 * D_GMLP + 3 * LANES
ZF_WI = (D_POOL + 2 * D_GMLP) // LANES
ZF_KA = ZF_WI + 1
ZF_KB = ZF_WI + 2

VMEM_LIMIT = 56 * 1024 * 1024


def _cparams(sem):
    return pltpu.CompilerParams(dimension_semantics=sem, vmem_limit_bytes=VMEM_LIMIT)


def _rms(x, g):
    ms = jnp.mean(x * x, axis=-1, keepdims=True)
    return x * lax.rsqrt(ms + EPS) * g


def _norm_matmul_kernel(x_ref, g_ref, w_ref, o_ref, h_ref):
    @pl.when(pl.program_id(1) == 0)
    def _():
        h_ref[...] = _rms(x_ref[...], g_ref[...]).astype(BF16)

    o_ref[...] = jnp.dot(h_ref[...], w_ref[...].astype(BF16), preferred_element_type=F32).astype(o_ref.dtype)


def _norm_matmul(x, g, w, layer, out_dtype, tm, tn, n, first_tile=0):
    m, d = x.shape
    return pl.pallas_call(
        _norm_matmul_kernel,
        grid=(m // tm, n // tn),
        in_specs=[
            pl.BlockSpec((tm, d), lambda i, j: (i, 0)),
            pl.BlockSpec((1, d), lambda i, j: (0, 0)),
            pl.BlockSpec((None, d, tn), lambda i, j: (layer, 0, first_tile + j)),
        ],
        out_specs=pl.BlockSpec((tm, tn), lambda i, j: (i, j)),
        out_shape=jax.ShapeDtypeStruct((m, n), out_dtype),
        scratch_shapes=[pltpu.VMEM((tm, d), BF16)],
        compiler_params=_cparams(("parallel", "arbitrary")),
        name="norm_in_proj",
    )(x, g, w)


def _gelu_tanh(x):
    return 0.5 * x * (1.0 + jnp.tanh(0.7978845608028654 * (x + 0.044715 * (x * x * x))))


def _mixer_kernel(a_ref, halo_ref, u_ref, v_ref, pw_ref, ps_ref, lg_ref, lb_ref, sw_ref, sb_ref,
                  o_ref, ext_ref, *, seq):
    tt = a_ref.shape[0]
    t0 = (pl.program_id(0) * tt) % seq
    a = a_ref[...]
    ext_ref[0:POOL_HALO, :] = jnp.where(t0 == 0, 0.0, halo_ref[...])
    ext_ref[POOL_HALO:, :] = a
    pos = (t0 + 1 + lax.broadcasted_iota(jnp.int32, (tt, 1), 0)).astype(F32)
    for g, w in enumerate(POOL_WINDOWS):
        cs = slice(g * HEAD_DIM, (g + 1) * HEAD_DIM)
        ag = a[:, cs]
        acc = ag
        for j in range(1, w):
            acc = acc + ext_ref[pl.ds(POOL_HALO - j, tt), cs]
        pooled = acc / jnp.minimum(pos, float(w)) - ag
        y = jnp.dot(pooled.astype(BF16), pw_ref[g], preferred_element_type=F32) * ps_ref[:, cs]
        o_ref[:, cs] = y.astype(o_ref.dtype)

    zu = _gelu_tanh(u_ref[...])
    zv = _gelu_tanh(v_ref[...])
    mu = jnp.mean(zv, axis=-1, keepdims=True)
    dv = zv - mu
    var = jnp.mean(dv * dv, axis=-1, keepdims=True)
    vn = (dv * lax.rsqrt(var + EPS) * lg_ref[...] + lb_ref[...]).astype(BF16)
    tril = (lax.broadcasted_iota(jnp.int32, (CHUNK, CHUNK), 0)
            >= lax.broadcasted_iota(jnp.int32, (CHUNK, CHUNK), 1))
    for g in range(GMLP_HEADS):
        cs = slice(g * HEAD_DIM, (g + 1) * HEAD_DIM)
        wg = jnp.where(tril, sw_ref[g], 0.0).astype(BF16)
        for c in range(tt // CHUNK):
            rs = slice(c * CHUNK, (c + 1) * CHUNK)
            mixed = jnp.dot(wg, vn[rs, cs], preferred_element_type=F32) + sb_ref[g]
            o_ref[rs, D_POOL + g * HEAD_DIM:D_POOL + (g + 1) * HEAD_DIM] = (zu[rs, cs] * mixed).astype(o_ref.dtype)


def _mixer(zf, pool_w, pool_scale, ln_g, ln_b, sgu_w, sgu_b_exp, seq, tt):
    m = zf.shape[0]
    hb = tt // POOL_HALO
    const3 = lambda i: (0, 0, 0)
    const2 = lambda i: (0, 0)
    return pl.pallas_call(
        functools.partial(_mixer_kernel, seq=seq),
        grid=(m // tt,),
        in_specs=[
            pl.BlockSpec((tt, D_POOL), lambda i: (i, 0)),
            pl.BlockSpec((POOL_HALO, D_POOL), lambda i: (jnp.maximum(i * hb - 1, 0), 0)),
            pl.BlockSpec((tt, D_GMLP), lambda i: (i, D_POOL // D_GMLP)),
            pl.BlockSpec((tt, D_GMLP), lambda i: (i, D_POOL // D_GMLP + 1)),
            pl.BlockSpec((len(POOL_WINDOWS), HEAD_DIM, HEAD_DIM), const3),
            pl.BlockSpec((1, D_POOL), const2),
            pl.BlockSpec((1, D_GMLP), const2),
            pl.BlockSpec((1, D_GMLP), const2),
            pl.BlockSpec((GMLP_HEADS, CHUNK, CHUNK), const3),
            pl.BlockSpec((GMLP_HEADS, CHUNK, HEAD_DIM), const3),
        ],
        out_specs=pl.BlockSpec((tt, D_POOL + D_GMLP), lambda i: (i, 0)),
        out_shape=jax.ShapeDtypeStruct((m, D_POOL + D_GMLP), BF16),
        scratch_shapes=[pltpu.VMEM((POOL_HALO + tt, D_POOL), F32)],
        compiler_params=_cparams(("parallel",)),
        name="pool_gmlp_mixer",
    )(zf, zf, zf, zf, pool_w, pool_scale, ln_g, ln_b, sgu_w, sgu_b_exp)


BISECT_CHECK_EVERY = 4
BISECT_MAX_ROUNDS = 10
SUBLANES = 8
N_PARTIAL = 4


def _fold_rows(x, op):
    rows, n = x.shape
    y = x.reshape(rows // (N_PARTIAL * SUBLANES), N_PARTIAL, SUBLANES, n)
    return op(op(y, axis=0), axis=0)


def _indexer_kernel(qi_ref, ka_ref, kb_ref, wi_ref, o_ref, score_ref, *, top_k, seq):
    nkc, kc, tq = score_ref.shape
    i = pl.program_id(1)
    nk = ((i + 1) * tq + kc - 1) // kc
    kf = float(top_k)
    nt = (((1,), (1,)), ((), ()))

    wts = (wi_ref[...] * (N_IDX_HEADS ** -0.5 * D_IDX ** -0.5)).T
    qpos = i * tq + lax.broadcasted_iota(jnp.int32, (1, tq), 1)

    def key_index(c, r=0, rows=None):
        return c * kc + r + lax.broadcasted_iota(jnp.int32, (kc if rows is None else rows, tq), 0)

    def score_chunk(c, carry, masked):
        mn, mx = carry
        k0 = pl.multiple_of(c * kc, kc)
        ka = ka_ref[pl.ds(k0, kc), :].astype(BF16)
        kb = kb_ref[pl.ds(k0, kc), :].astype(BF16)
        sc = jnp.zeros((kc, tq), F32)
        for p in range(N_IDX_HEADS // 2):
            qp = qi_ref[:, p * LANES:(p + 1) * LANES]
            for half, kk in enumerate((ka, kb)):
                h = 2 * p + half
                lg = lax.dot_general(kk, qp, nt, preferred_element_type=F32)
                sc = sc + jnp.maximum(lg, 0.0) * wts[h:h + 1, :]
        if masked:
            causal = key_index(c) <= qpos
            sc_hi = jnp.where(causal, sc, jnp.inf)
            sc = jnp.where(causal, sc, -jnp.inf)
        else:
            sc_hi = sc
        score_ref[c] = sc
        return jnp.minimum(mn, _fold_rows(sc_hi, jnp.min)), jnp.maximum(mx, _fold_rows(sc, jnp.max))

    ext = (jnp.full((SUBLANES, tq), jnp.inf, F32), jnp.full((SUBLANES, tq), -jnp.inf, F32))
    ext = lax.fori_loop(0, nk - 1, functools.partial(score_chunk, masked=False), ext)
    mn, mx = score_chunk(nk - 1, ext, masked=True)
    lo = jnp.min(mn, axis=0, keepdims=True)
    hi = jnp.max(mx, axis=0, keepdims=True)

    def count(pred):
        def body(c, accs):
            accs = list(accs)
            for r in range(0, kc, SUBLANES):
                hit = pred(score_ref[c, r:r + SUBLANES, :], c, r)
                accs[(r // SUBLANES) % N_PARTIAL] += jnp.where(hit, 1.0, 0.0)
            return tuple(accs)

        accs = lax.fori_loop(0, nk, body, tuple(jnp.zeros((SUBLANES, tq), F32) for _ in range(N_PARTIAL)))
        return jnp.sum(functools.reduce(lambda a, b: a + b, accs), axis=0, keepdims=True)

    c_hi = count(lambda s, c, r: s >= hi)
    c_lo = (qpos + 1).astype(F32)

    def bisect_step(_, st):
        lo, hi, c_lo, c_hi = st
        mid = 0.5 * lo + 0.5 * hi
        cnt = count(lambda s, c, r: s >= mid)
        ge = cnt >= kf
        return jnp.where(ge, mid, lo), jnp.where(ge, hi, mid), jnp.where(ge, cnt, c_lo), jnp.where(ge, c_hi, cnt)

    def bisect_round(carry):
        r, st = carry
        return r + 1, lax.fori_loop(0, BISECT_CHECK_EVERY, bisect_step, st)

    def bisect_more(carry):
        r, (lo, hi, c_lo, c_hi) = carry
        return (r < BISECT_MAX_ROUNDS) & (jnp.max(c_lo) > kf)

    _, (lo, hi, c_lo, c_hi) = lax.while_loop(bisect_more, bisect_round, (0, (lo, hi, c_lo, c_hi)))

    at_max = c_hi >= kf
    thr = jnp.where(at_max, hi, lo)
    c_gt = jnp.where(at_max, 0.0, c_hi)
    c_thr = jnp.where(at_max, c_hi, c_lo)
    excess = c_thr > kf
    any_excess = jnp.max(jnp.where(excess, 1.0, 0.0)) > 0.0
    eye = (lax.broadcasted_iota(jnp.int32, (tq, tq), 0) == lax.broadcasted_iota(jnp.int32, (tq, tq), 1)).astype(BF16)

    def emit(c, sel):
        mask_t = jnp.where(sel, 0.0, NEG_BIAS).astype(BF16)
        o_ref[c] = lax.dot_general(eye, mask_t, nt, preferred_element_type=F32).astype(o_ref.dtype)

    @pl.when(jnp.logical_not(any_excess))
    def _():
        def body(c, _):
            emit(c, score_ref[c] >= thr)
            return 0

        lax.fori_loop(0, nk, body, 0)

    @pl.when(any_excess)
    def _():
        upper = jnp.where(at_max, jnp.inf, hi)
        need = kf - c_gt

        def jbisect(_, carry):
            jlo, jhi = carry
            jmid = (jlo + jhi) >> 1
            cnt = count(lambda s, c, r: (s >= thr) & (s < upper) & (key_index(c, r, SUBLANES) <= jmid))
            ge = cnt >= need
            return jnp.where(ge, jlo, jmid), jnp.where(ge, jmid, jhi)

        _, jhi = lax.fori_loop(0, seq.bit_length(), jbisect,
                               (jnp.full((1, tq), -1, jnp.int32), jnp.full((1, tq), seq - 1, jnp.int32)))
        jstar = jnp.where(excess, jhi, seq)

        def body(c, _):
            s = score_ref[c]
            emit(c, (s >= thr) & ((s >= upper) | (key_index(c) <= jstar)))
            return 0

        lax.fori_loop(0, nk, body, 0)

    def fill(c, _):
        o_ref[c] = jnp.full((tq, kc), NEG_BIAS, o_ref.dtype)
        return 0

    lax.fori_loop(nk, nkc, fill, 0)


def _indexer(zb, zf, batch, seq, tq, kc):
    nq = seq // tq
    nkc = seq // kc
    top_k = min(TOPK_MAX, seq // 4)
    return pl.pallas_call(
        functools.partial(_indexer_kernel, top_k=top_k, seq=seq),
        grid=(batch, nq),
        in_specs=[
            pl.BlockSpec((tq, D_ATTN), lambda b, i: (b * nq + i, ZB_QI)),
            pl.BlockSpec((seq, LANES), lambda b, i: (b, ZF_KA)),
            pl.BlockSpec((seq, LANES), lambda b, i: (b, ZF_KB)),
            pl.BlockSpec((tq, LANES), lambda b, i: (b * nq + i, ZF_WI)),
        ],
        out_specs=pl.BlockSpec((None, nkc, tq, kc), lambda b, i: (b * nq + i, 0, 0, 0)),
        out_shape=jax.ShapeDtypeStruct((batch * nq, nkc, tq, kc), BF16),
        scratch_shapes=[pltpu.VMEM((nkc, kc, tq), F32)],
        compiler_params=_cparams(("parallel", "arbitrary")),
        name="dsa_indexer_topk",
    )(zb, zf, zf, zf)


ATTN_ROW_BLOCK = 64
LOG2E = 1.4426950408889634


def _attn_kernel(qt_ref, ct_ref, lt_ref, q_ref, k_ref, v_ref, b_ref, o_ref,
                 m_ref, l_ref, acc_ref, s_ref, p_ref, bias_ref, alpha_ref):
    tq = q_ref.shape[0]
    kc = k_ref.shape[0]
    ncol = kc // LANES
    step = pl.program_id(1)
    nt = (((1,), (1,)), ((), ()))

    @pl.when(ct_ref[step] == 0)
    def _():
        m_ref[...] = jnp.full(m_ref.shape, -jnp.inf, F32)
        l_ref[...] = jnp.zeros(l_ref.shape, F32)
        acc_ref[...] = jnp.zeros(acc_ref.shape, F32)

    bias_ref[...] = b_ref[...].astype(F32)

    def qk(h):
        cs = slice(h * HEAD_DIM, (h + 1) * HEAD_DIM)
        s_ref[h % 2] = lax.dot_general(q_ref[:, cs], k_ref[:, cs], nt, preferred_element_type=F32)

    qk(0)
    for h in range(N_ATTN_HEADS):
        cs = slice(h * HEAD_DIM, (h + 1) * HEAD_DIM)
        buf = h % 2
        if h + 1 < N_ATTN_HEADS:
            qk(h + 1)
        for r in range(tq // ATTN_ROW_BLOCK):
            rows = slice(r * ATTN_ROW_BLOCK, (r + 1) * ATTN_ROW_BLOCK)
            t = [s_ref[buf, rows, j * LANES:(j + 1) * LANES] * (HEAD_DIM ** -0.5 * LOG2E)
                 + bias_ref[rows, j * LANES:(j + 1) * LANES] for j in range(ncol)]
            tmax = t[0]
            for j in range(1, ncol):
                tmax = jnp.maximum(tmax, t[j])
            m_prev = m_ref[h, rows, :]
            m_new = jnp.maximum(m_prev, jnp.max(tmax, axis=-1, keepdims=True))
            alpha = jnp.exp2(m_prev - m_new)
            p = [jnp.exp2(t[j] - m_new) for j in range(ncol)]
            psum = p[0]
            for j in range(1, ncol):
                psum = psum + p[j]
            l_ref[h, rows, :] = alpha * l_ref[h, rows, :] + jnp.sum(psum, axis=-1, keepdims=True)
            m_ref[h, rows, :] = m_new
            alpha_ref[buf, rows, :] = alpha
            for j in range(ncol):
                p_ref[buf, rows, j * LANES:(j + 1) * LANES] = p[j].astype(BF16)
        acc_ref[:, cs] = alpha_ref[buf] * acc_ref[:, cs] + jnp.dot(p_ref[buf], v_ref[:, cs],
                                                                 preferred_element_type=F32)

    @pl.when(lt_ref[step] == 1)
    def _():
        for h in range(N_ATTN_HEADS):
            cs = slice(h * HEAD_DIM, (h + 1) * HEAD_DIM)
            o_ref[:, cs] = (acc_ref[:, cs] / l_ref[h]).astype(o_ref.dtype)


def _attention(zb, bias, batch, seq, tq, kc):
    nq = seq // tq
    nkc = seq // kc
    pairs = [(i, c) for i in range(nq) for c in range(((i + 1) * tq - 1) // kc + 1)]
    q_tab = jnp.asarray([i for i, _ in pairs], jnp.int32)
    c_tab = jnp.asarray([c for _, c in pairs], jnp.int32)
    l_tab = jnp.asarray([int(n + 1 == len(pairs) or pairs[n + 1][0] != i) for n, (i, _) in enumerate(pairs)],
                        jnp.int32)
    grid_spec = pltpu.PrefetchScalarGridSpec(
        num_scalar_prefetch=3,
        grid=(batch, len(pairs)),
        in_specs=[
            pl.BlockSpec((tq, D_ATTN), lambda b, t, qt, ct, lt: (b * nq + qt[t], ZB_Q)),
            pl.BlockSpec((kc, D_ATTN), lambda b, t, qt, ct, lt: (b * nkc + ct[t], ZB_K)),
            pl.BlockSpec((kc, D_ATTN), lambda b, t, qt, ct, lt: (b * nkc + ct[t], ZB_V)),
            pl.BlockSpec((None, None, tq, kc), lambda b, t, qt, ct, lt: (b * nq + qt[t], ct[t], 0, 0)),
        ],
        out_specs=pl.BlockSpec((tq, D_ATTN), lambda b, t, qt, ct, lt: (b * nq + qt[t], 0)),
        scratch_shapes=[
            pltpu.VMEM((N_ATTN_HEADS, tq, LANES), F32),
            pltpu.VMEM((N_ATTN_HEADS, tq, LANES), F32),
            pltpu.VMEM((tq, D_ATTN), F32),
            pltpu.VMEM((2, tq, kc), F32),
            pltpu.VMEM((2, tq, kc), BF16),
            pltpu.VMEM((tq, kc), F32),
            pltpu.VMEM((2, tq, LANES), F32),
        ],
    )
    return pl.pallas_call(
        _attn_kernel,
        grid_spec=grid_spec,
        out_shape=jax.ShapeDtypeStruct((batch * seq, D_ATTN), BF16),
        compiler_params=_cparams(("parallel", "arbitrary")),
        name="dsa_masked_attention",
    )(q_tab, c_tab, l_tab, zb, zb, zb, bias)


def _out_proj_kernel(ypg_ref, ya_ref, w_ref, x_ref, g_ref, o_ref):
    mix = jnp.dot(ypg_ref[:, 0:D_POOL], w_ref[0:D_POOL, :], preferred_element_type=F32)
    mix = mix + jnp.dot(ya_ref[...], w_ref[D_POOL:D_POOL + D_ATTN, :], preferred_element_type=F32)
    mix = mix + jnp.dot(ypg_ref[:, D_POOL:], w_ref[D_POOL + D_ATTN:, :], preferred_element_type=F32)
    o_ref[...] = x_ref[...] + _rms(mix, g_ref[...])


def _out_proj(ypg, ya, w, layer, x, g, tm):
    m, d = x.shape
    return pl.pallas_call(
        _out_proj_kernel,
        grid=(m // tm,),
        in_specs=[
            pl.BlockSpec((tm, ypg.shape[1]), lambda i: (i, 0)),
            pl.BlockSpec((tm, ya.shape[1]), lambda i: (i, 0)),
            pl.BlockSpec((None,) + w.shape[1:], lambda i: (layer, 0, 0)),
            pl.BlockSpec((tm, d), lambda i: (i, 0)),
            pl.BlockSpec((1, d), lambda i: (0, 0)),
        ],
        out_specs=pl.BlockSpec((tm, d), lambda i: (i, 0)),
        out_shape=jax.ShapeDtypeStruct((m, d), F32),
        compiler_params=_cparams(("parallel",)),
        name="out_proj_residual",
    )(ypg, ya, w, x, g)


def _ffn_kernel(x_ref, xh_ref, g1_ref, wg_ref, wv_ref, cwg_ref, cwv_ref, cbg_ref, cbv_ref, wd_ref, g2_ref,
                o_ref, h_ref, up_ref, acc_ref, *, seq):
    tm = x_ref.shape[0]
    j = pl.program_id(1)
    at_start = (pl.program_id(0) * tm) % seq == 0

    @pl.when(j == 0)
    def _():
        h_ref[0:CONV_HALO, :] = _rms(xh_ref[...], g1_ref[...]).astype(BF16)
        h_ref[CONV_HALO:, :] = _rms(x_ref[...], g1_ref[...]).astype(BF16)
        acc_ref[...] = jnp.zeros(acc_ref.shape, F32)

    def conv(w_ref, cw_ref, cb_ref):
        up = jnp.dot(h_ref[...], w_ref[...], preferred_element_type=F32)
        up_ref[0:CONV_HALO, :] = jnp.where(at_start, 0.0, up[0:CONV_HALO, :])
        up_ref[CONV_HALO:, :] = up[CONV_HALO:, :]
        out = cb_ref[...] + cw_ref[CONV_W - 1:CONV_W, :] * up[CONV_HALO:, :]
        for t in range(CONV_W - 1):
            shift = CONV_W - 1 - t
            out = out + cw_ref[t:t + 1, :] * up_ref[pl.ds(CONV_HALO - shift, tm), :]
        return out

    gate = conv(wg_ref, cwg_ref, cbg_ref)
    act = gate * (1.0 / (1.0 + jnp.exp(-gate)))
    val = conv(wv_ref, cwv_ref, cbv_ref)
    acc_ref[...] += jnp.dot((act * val).astype(BF16), wd_ref[...], preferred_element_type=F32)

    @pl.when(j == pl.num_programs(1) - 1)
    def _():
        o_ref[...] = x_ref[...] + _rms(acc_ref[...], g2_ref[...])


def _conv_ffn(x, g1, w_up, conv_w, conv_b, w_down, layer, g2, seq, tm, tf):
    m, d = x.shape
    nf = D_FF // tf
    hb = tm // CONV_HALO
    return pl.pallas_call(
        functools.partial(_ffn_kernel, seq=seq),
        grid=(m // tm, nf),
        in_specs=[
            pl.BlockSpec((tm, d), lambda i, j: (i, 0)),
            pl.BlockSpec((CONV_HALO, d), lambda i, j: (jnp.maximum(i * hb - 1, 0), 0)),
            pl.BlockSpec((1, d), lambda i, j: (0, 0)),
            pl.BlockSpec((None, d, tf), lambda i, j: (layer, 0, j)),
            pl.BlockSpec((None, d, tf), lambda i, j: (layer, 0, nf + j)),
            pl.BlockSpec((CONV_W, tf), lambda i, j: (0, j)),
            pl.BlockSpec((CONV_W, tf), lambda i, j: (0, nf + j)),
            pl.BlockSpec((1, tf), lambda i, j: (0, j)),
            pl.BlockSpec((1, tf), lambda i, j: (0, nf + j)),
            pl.BlockSpec((None, tf, d), lambda i, j: (layer, j, 0)),
            pl.BlockSpec((1, d), lambda i, j: (0, 0)),
        ],
        out_specs=pl.BlockSpec((tm, d), lambda i, j: (i, 0)),
        out_shape=jax.ShapeDtypeStruct((m, d), F32),
        scratch_shapes=[
            pltpu.VMEM((CONV_HALO + tm, d), BF16),
            pltpu.VMEM((CONV_HALO + tm, tf), F32),
            pltpu.VMEM((tm, d), F32),
        ],
        compiler_params=_cparams(("parallel", "arbitrary")),
        name="conv_ffn_residual",
    )(x, x, g1, w_up, w_up, conv_w, conv_w, conv_b, conv_b, w_down, g2)


def _zf_weights(w):
    o_ki = D_POOL + 3 * D_ATTN + N_IDX_HEADS * D_IDX
    o_wi = o_ki + D_IDX
    o_g = o_wi + N_IDX_HEADS
    ki = w[:, :, o_ki:o_wi]
    zk = jnp.zeros_like(ki)
    zw = jnp.zeros(w.shape[:2] + (LANES - N_IDX_HEADS,), w.dtype)
    wf = jnp.concatenate([w[:, :, 0:D_POOL], w[:, :, o_g:], w[:, :, o_wi:o_g], zw, ki, zk, zk, ki], axis=2)
    return wf.astype(BF16)


def _trunk(x, w_in, pool_w, pool_scale, sgu_ln_g, sgu_ln_b, sgu_w, sgu_b, w_out, ffn_w_up, ffn_conv_w,
           ffn_conv_b, ffn_w_down, norm_pre_mix, norm_post_mix, norm_pre_ffn, norm_post_ffn,
           *, tm, tq, kc, tf):
    batch, seq, d = x.shape
    depth = w_in.shape[0]
    xf = x.reshape(batch * seq, d)
    row = lambda v: v.reshape(1, -1)
    wf = _zf_weights(w_in)
    wo = w_out.astype(BF16)
    w_up = ffn_w_up.astype(BF16)
    w_down = ffn_w_down.astype(BF16)
    pw = pool_w.astype(BF16)
    for l in range(depth):
        g = row(norm_pre_mix[l])
        zb = _norm_matmul(xf, g, w_in, l, BF16, 2 * tm, ZB_TN, ZB_WIDTH, first_tile=D_POOL // ZB_TN)
        zf = _norm_matmul(xf, g, wf, l, F32, tm, ZF_WIDTH, ZF_WIDTH)
        sb_exp = jnp.broadcast_to(sgu_b[l][:, :, None], (GMLP_HEADS, CHUNK, HEAD_DIM))
        ypg = _mixer(zf, pw[l], row(pool_scale[l]), row(sgu_ln_g[l]), row(sgu_ln_b[l]), sgu_w[l], sb_exp, seq, tm)
        bias = _indexer(zb, zf, batch, seq, tq, kc)
        ya = _attention(zb, bias, batch, seq, tq, kc)
        xf = _out_proj(ypg, ya, wo, l, xf, row(norm_post_mix[l]), tm)
        xf = _conv_ffn(xf, row(norm_pre_ffn[l]), w_up, ffn_conv_w[l], row(ffn_conv_b[l]), w_down, l,
                       row(norm_post_ffn[l]), seq, tm, tf)
    return xf.reshape(batch, seq, d)


def kernel(x, w_in, pool_w, pool_scale, sgu_ln_g, sgu_ln_b, sgu_w, sgu_b, w_out, ffn_w_up, ffn_conv_w,
           ffn_conv_b, ffn_w_down, norm_pre_mix, norm_post_mix, norm_pre_ffn, norm_post_ffn):
    return _trunk(x, w_in, pool_w, pool_scale, sgu_ln_g, sgu_ln_b, sgu_w, sgu_b, w_out, ffn_w_up, ffn_conv_w,
                  ffn_conv_b, ffn_w_down, norm_pre_mix, norm_post_mix, norm_pre_ffn, norm_post_ffn,
                  tm=512, tq=256, kc=512, tf=512)
```

```python
import functools

import jax
import jax.numpy as jnp
from jax import lax
from jax.experimental import pallas as pl
from jax.experimental.pallas import tpu as pltpu

F32 = jnp.float32
BF16 = jnp.bfloat16

D_MODEL = 2048
HEAD_DIM = 128
POOL_WINDOWS = (2, 4, 8, 16)
D_POOL = len(POOL_WINDOWS) * HEAD_DIM
POOL_HALO = 16
N_ATTN_HEADS = 8
D_ATTN = N_ATTN_HEADS * HEAD_DIM
N_IDX_HEADS = 16
D_IDX = 64
TOPK_MAX = 256
GMLP_HEADS = 4
D_GMLP = GMLP_HEADS * HEAD_DIM
CHUNK = 128
D_FF = 5632
CONV_W = 3
CONV_HALO = 8
EPS = 1e-6

LANES = 128
NEG_BIAS = -1e30

ZB_WIDTH = 4 * D_ATTN + 2 * LANES
ZB_Q, ZB_K, ZB_V, ZB_QI = 0, 1, 2, 3
ZB_KA = 4 * D_ATTN // LANES
ZB_KB = ZB_KA + 1
ZF_WIDTH = D_POOL + 2 * D_GMLP + LANES
ZF_WI = (D_POOL + 2 * D_GMLP) // LANES

VMEM_LIMIT = 56 * 1024 * 1024


def _cparams(sem):
    return pltpu.CompilerParams(dimension_semantics=sem, vmem_limit_bytes=VMEM_LIMIT)


def _rms(x, g):
    ms = jnp.mean(x * x, axis=-1, keepdims=True)
    return x * lax.rsqrt(ms + EPS) * g


def _norm_matmul_kernel(x_ref, g_ref, w_ref, o_ref, h_ref):
    @pl.when(pl.program_id(1) == 0)
    def _():
        h_ref[...] = _rms(x_ref[...], g_ref[...]).astype(BF16)

    o_ref[...] = jnp.dot(h_ref[...], w_ref[...], preferred_element_type=F32).astype(o_ref.dtype)


def _norm_matmul(x, g, w, layer, out_dtype, tm, tn):
    m, d = x.shape
    n = w.shape[2]
    return pl.pallas_call(
        _norm_matmul_kernel,
        grid=(m // tm, n // tn),
        in_specs=[
            pl.BlockSpec((tm, d), lambda i, j: (i, 0)),
            pl.BlockSpec((1, d), lambda i, j: (0, 0)),
            pl.BlockSpec((None, d, tn), lambda i, j: (layer, 0, j)),
        ],
        out_specs=pl.BlockSpec((tm, tn), lambda i, j: (i, j)),
        out_shape=jax.ShapeDtypeStruct((m, n), out_dtype),
        scratch_shapes=[pltpu.VMEM((tm, d), BF16)],
        compiler_params=_cparams(("parallel", "arbitrary")),
        name="norm_in_proj",
    )(x, g, w)


def _gelu_tanh(x):
    return 0.5 * x * (1.0 + jnp.tanh(0.7978845608028654 * (x + 0.044715 * (x * x * x))))


def _mixer_kernel(a_ref, halo_ref, u_ref, v_ref, pw_ref, ps_ref, lg_ref, lb_ref, sw_ref, sb_ref,
                  o_ref, ext_ref, *, seq):
    tt = a_ref.shape[0]
    t0 = (pl.program_id(0) * tt) % seq
    a = a_ref[...]
    ext_ref[0:POOL_HALO, :] = jnp.where(t0 == 0, 0.0, halo_ref[...])
    ext_ref[POOL_HALO:, :] = a
    pos = (t0 + 1 + lax.broadcasted_iota(jnp.int32, (tt, 1), 0)).astype(F32)
    for g, w in enumerate(POOL_WINDOWS):
        cs = slice(g * HEAD_DIM, (g + 1) * HEAD_DIM)
        ag = a[:, cs]
        acc = ag
        for j in range(1, w):
            acc = acc + ext_ref[pl.ds(POOL_HALO - j, tt), cs]
        pooled = acc / jnp.minimum(pos, float(w)) - ag
        y = jnp.dot(pooled.astype(BF16), pw_ref[g], preferred_element_type=F32) * ps_ref[:, cs]
        o_ref[:, cs] = y.astype(o_ref.dtype)

    zu = _gelu_tanh(u_ref[...])
    zv = _gelu_tanh(v_ref[...])
    mu = jnp.mean(zv, axis=-1, keepdims=True)
    dv = zv - mu
    var = jnp.mean(dv * dv, axis=-1, keepdims=True)
    vn = (dv * lax.rsqrt(var + EPS) * lg_ref[...] + lb_ref[...]).astype(BF16)
    tril = (lax.broadcasted_iota(jnp.int32, (CHUNK, CHUNK), 0)
            >= lax.broadcasted_iota(jnp.int32, (CHUNK, CHUNK), 1))
    for g in range(GMLP_HEADS):
        cs = slice(g * HEAD_DIM, (g + 1) * HEAD_DIM)
        wg = jnp.where(tril, sw_ref[g], 0.0).astype(BF16)
        for c in range(tt // CHUNK):
            rs = slice(c * CHUNK, (c + 1) * CHUNK)
            mixed = jnp.dot(wg, vn[rs, cs], preferred_element_type=F32) + sb_ref[g]
            o_ref[rs, D_POOL + g * HEAD_DIM:D_POOL + (g + 1) * HEAD_DIM] = (zu[rs, cs] * mixed).astype(o_ref.dtype)


def _mixer(zf, pool_w, pool_scale, ln_g, ln_b, sgu_w, sgu_b_exp, seq, tt):
    m = zf.shape[0]
    hb = tt // POOL_HALO
    const3 = lambda i: (0, 0, 0)
    const2 = lambda i: (0, 0)
    return pl.pallas_call(
        functools.partial(_mixer_kernel, seq=seq),
        grid=(m // tt,),
        in_specs=[
            pl.BlockSpec((tt, D_POOL), lambda i: (i, 0)),
            pl.BlockSpec((POOL_HALO, D_POOL), lambda i: (jnp.maximum(i * hb - 1, 0), 0)),
            pl.BlockSpec((tt, D_GMLP), lambda i: (i, D_POOL // D_GMLP)),
            pl.BlockSpec((tt, D_GMLP), lambda i: (i, D_POOL // D_GMLP + 1)),
            pl.BlockSpec((len(POOL_WINDOWS), HEAD_DIM, HEAD_DIM), const3),
            pl.BlockSpec((1, D_POOL), const2),
            pl.BlockSpec((1, D_GMLP), const2),
            pl.BlockSpec((1, D_GMLP), const2),
            pl.BlockSpec((GMLP_HEADS, CHUNK, CHUNK), const3),
            pl.BlockSpec((GMLP_HEADS, CHUNK, HEAD_DIM), const3),
        ],
        out_specs=pl.BlockSpec((tt, D_POOL + D_GMLP), lambda i: (i, 0)),
        out_shape=jax.ShapeDtypeStruct((m, D_POOL + D_GMLP), BF16),
        scratch_shapes=[pltpu.VMEM((POOL_HALO + tt, D_POOL), F32)],
        compiler_params=_cparams(("parallel",)),
        name="pool_gmlp_mixer",
    )(zf, zf, zf, zf, pool_w, pool_scale, ln_g, ln_b, sgu_w, sgu_b_exp)


BISECT_CHECK_EVERY = 4
BISECT_MAX_ROUNDS = 10
SUBLANES = 8
N_PARTIAL = 4
LOG2E = 1.4426950408889634


def _fold_rows(x, op):
    rows, n = x.shape
    y = x.reshape(rows // (N_PARTIAL * SUBLANES), N_PARTIAL, SUBLANES, n)
    return op(op(y, axis=0), axis=0)


def _dsa_kernel(qi_ref, q_ref, ka_ref, kb_ref, k_ref, vt_ref, wi_ref, o_ref,
                score_ref, bias_ref, sel_ref, m_ref, l_ref, acc_ref, *, top_k, seq):
    nkc, kc, tq = score_ref.shape
    i = pl.program_id(1)
    nk = ((i + 1) * tq + kc - 1) // kc
    kf = float(top_k)
    nt = (((1,), (1,)), ((), ()))

    wts = (wi_ref[...] * (N_IDX_HEADS ** -0.5 * D_IDX ** -0.5)).T
    qpos = i * tq + lax.broadcasted_iota(jnp.int32, (1, tq), 1)

    def key_index(c, r=0, rows=None):
        return c * kc + r + lax.broadcasted_iota(jnp.int32, (kc if rows is None else rows, tq), 0)

    def score_chunk(c, carry, masked):
        mn, mx = carry
        k0 = pl.multiple_of(c * kc, kc)
        ka = ka_ref[pl.ds(k0, kc), :]
        kb = kb_ref[pl.ds(k0, kc), :]
        sc = jnp.zeros((kc, tq), F32)
        for p in range(N_IDX_HEADS // 2):
            qp = qi_ref[:, p * LANES:(p + 1) * LANES]
            for half, kk in enumerate((ka, kb)):
                h = 2 * p + half
                lg = lax.dot_general(kk, qp, nt, preferred_element_type=F32)
                sc = sc + jnp.maximum(lg, 0.0) * wts[h:h + 1, :]
        if masked:
            causal = key_index(c) <= qpos
            sc_hi = jnp.where(causal, sc, jnp.inf)
            sc = jnp.where(causal, sc, -jnp.inf)
        else:
            sc_hi = sc
        score_ref[c] = sc
        return jnp.minimum(mn, _fold_rows(sc_hi, jnp.min)), jnp.maximum(mx, _fold_rows(sc, jnp.max))

    ext = (jnp.full((SUBLANES, tq), jnp.inf, F32), jnp.full((SUBLANES, tq), -jnp.inf, F32))
    ext = lax.fori_loop(0, nk - 1, functools.partial(score_chunk, masked=False), ext)
    mn, mx = score_chunk(nk - 1, ext, masked=True)
    lo = jnp.min(mn, axis=0, keepdims=True)
    hi = jnp.max(mx, axis=0, keepdims=True)

    def count(pred):
        def body(c, accs):
            accs = list(accs)
            for r in range(0, kc, SUBLANES):
                hit = pred(score_ref[c, r:r + SUBLANES, :], c, r)
                accs[(r // SUBLANES) % N_PARTIAL] += jnp.where(hit, 1.0, 0.0)
            return tuple(accs)

        accs = lax.fori_loop(0, nk, body, tuple(jnp.zeros((SUBLANES, tq), F32) for _ in range(N_PARTIAL)))
        return jnp.sum(functools.reduce(lambda a, b: a + b, accs), axis=0, keepdims=True)

    c_hi = count(lambda s, c, r: s >= hi)
    c_lo = (qpos + 1).astype(F32)

    def bisect_step(_, st):
        lo, hi, c_lo, c_hi = st
        mid = 0.5 * lo + 0.5 * hi
        cnt = count(lambda s, c, r: s >= mid)
        ge = cnt >= kf
        return jnp.where(ge, mid, lo), jnp.where(ge, hi, mid), jnp.where(ge, cnt, c_lo), jnp.where(ge, c_hi, cnt)

    def bisect_round(carry):
        r, st = carry
        return r + 1, lax.fori_loop(0, BISECT_CHECK_EVERY, bisect_step, st)

    def bisect_more(carry):
        r, (lo, hi, c_lo, c_hi) = carry
        return (r < BISECT_MAX_ROUNDS) & (jnp.max(c_lo) > kf)

    _, (lo, hi, c_lo, c_hi) = lax.while_loop(bisect_more, bisect_round, (0, (lo, hi, c_lo, c_hi)))

    at_max = c_hi >= kf
    thr = jnp.where(at_max, hi, lo)
    upper = jnp.where(at_max, jnp.inf, hi)
    c_gt = jnp.where(at_max, 0.0, c_hi)
    c_thr = jnp.where(at_max, c_hi, c_lo)
    excess = c_thr > kf
    sel_ref[...] = jnp.full(sel_ref.shape, seq, jnp.int32)

    @pl.when(jnp.max(jnp.where(excess, 1.0, 0.0)) > 0.0)
    def _():
        need = kf - c_gt

        def jbisect(_, carry):
            jlo, jhi = carry
            jmid = (jlo + jhi) >> 1
            cnt = count(lambda s, c, r: (s >= thr) & (s < upper) & (key_index(c, r, SUBLANES) <= jmid))
            ge = cnt >= need
            return jnp.where(ge, jlo, jmid), jnp.where(ge, jmid, jhi)

        _, jhi = lax.fori_loop(0, seq.bit_length(), jbisect,
                               (jnp.full((1, tq), -1, jnp.int32), jnp.full((1, tq), seq - 1, jnp.int32)))
        sel_ref[...] = jnp.broadcast_to(jnp.where(excess, jhi, seq), sel_ref.shape)

    jstar = sel_ref[0:1, :]

    m_ref[...] = jnp.full(m_ref.shape, -jnp.inf, F32)
    l_ref[...] = jnp.zeros(l_ref.shape, F32)
    acc_ref[...] = jnp.zeros(acc_ref.shape, F32)

    def attend(c, _):
        k0 = pl.multiple_of(c * kc, kc)
        s_idx = score_ref[c]
        keep = (s_idx >= thr) & ((s_idx >= upper) | (key_index(c) <= jstar))
        bias_ref[...] = jnp.where(keep, 0.0, NEG_BIAS)

        def qk(h):
            cs = slice(h * HEAD_DIM, (h + 1) * HEAD_DIM)
            return lax.dot_general(k_ref[pl.ds(k0, kc), cs], q_ref[:, cs], nt, preferred_element_type=F32)

        s_next = qk(0)
        for h in range(N_ATTN_HEADS):
            cs = slice(h * HEAD_DIM, (h + 1) * HEAD_DIM)
            s = s_next
            if h + 1 < N_ATTN_HEADS:
                s_next = qk(h + 1)
            t = s * (HEAD_DIM ** -0.5 * LOG2E) + bias_ref[...]
            m_prev = m_ref[h, 0:1, :]
            m_new = jnp.maximum(m_prev, jnp.max(_fold_rows(t, jnp.max), axis=0, keepdims=True))
            alpha = jnp.exp2(m_prev - m_new)
            p = jnp.exp2(t - m_new)
            l_new = alpha * l_ref[h, 0:1, :] + jnp.sum(_fold_rows(p, jnp.sum), axis=0, keepdims=True)
            m_ref[h] = jnp.broadcast_to(m_new, (SUBLANES, tq))
            l_ref[h] = jnp.broadcast_to(l_new, (SUBLANES, tq))
            pv = jnp.dot(vt_ref[c, cs, :], p.astype(BF16), preferred_element_type=F32)
            acc_ref[cs, :] = alpha * acc_ref[cs, :] + pv
        return 0

    lax.fori_loop(0, nk, attend, 0)

    for h in range(N_ATTN_HEADS):
        cs = slice(h * HEAD_DIM, (h + 1) * HEAD_DIM)
        o_ref[:, cs] = (acc_ref[cs, :] / l_ref[h, 0:1, :]).T.astype(o_ref.dtype)


def _dsa(zb, zf, vt, batch, seq, tq, kc):
    nq = seq // tq
    nkc = seq // kc
    top_k = min(TOPK_MAX, seq // 4)
    return pl.pallas_call(
        functools.partial(_dsa_kernel, top_k=top_k, seq=seq),
        grid=(batch, nq),
        in_specs=[
            pl.BlockSpec((tq, D_ATTN), lambda b, i: (b * nq + i, ZB_QI)),
            pl.BlockSpec((tq, D_ATTN), lambda b, i: (b * nq + i, ZB_Q)),
            pl.BlockSpec((seq, LANES), lambda b, i: (b, ZB_KA)),
            pl.BlockSpec((seq, LANES), lambda b, i: (b, ZB_KB)),
            pl.BlockSpec((seq, D_ATTN), lambda b, i: (b, ZB_K)),
            pl.BlockSpec((nkc, D_ATTN, kc), lambda b, i: (b, 0, 0)),
            pl.BlockSpec((tq, LANES), lambda b, i: (b * nq + i, ZF_WI)),
        ],
        out_specs=pl.BlockSpec((tq, D_ATTN), lambda b, i: (b * nq + i, 0)),
        out_shape=jax.ShapeDtypeStruct((batch * seq, D_ATTN), BF16),
        scratch_shapes=[
            pltpu.VMEM((nkc, kc, tq), F32),
            pltpu.VMEM((kc, tq), F32),
            pltpu.VMEM((SUBLANES, tq), jnp.int32),
            pltpu.VMEM((N_ATTN_HEADS, SUBLANES, tq), F32),
            pltpu.VMEM((N_ATTN_HEADS, SUBLANES, tq), F32),
            pltpu.VMEM((D_ATTN, tq), F32),
        ],
        compiler_params=_cparams(("arbitrary", "arbitrary")),
        name="dsa_index_select_attend",
    )(zb, zb, zb, zb, zb, vt, zf)


def _out_proj_kernel(ypg_ref, ya_ref, w_ref, x_ref, g_ref, o_ref):
    mix = jnp.dot(ypg_ref[:, 0:D_POOL], w_ref[0:D_POOL, :], preferred_element_type=F32)
    mix = mix + jnp.dot(ya_ref[...], w_ref[D_POOL:D_POOL + D_ATTN, :], preferred_element_type=F32)
    mix = mix + jnp.dot(ypg_ref[:, D_POOL:], w_ref[D_POOL + D_ATTN:, :], preferred_element_type=F32)
    o_ref[...] = x_ref[...] + _rms(mix, g_ref[...])


def _out_proj(ypg, ya, w, layer, x, g, tm):
    m, d = x.shape
    return pl.pallas_call(
        _out_proj_kernel,
        grid=(m // tm,),
        in_specs=[
            pl.BlockSpec((tm, ypg.shape[1]), lambda i: (i, 0)),
            pl.BlockSpec((tm, ya.shape[1]), lambda i: (i, 0)),
            pl.BlockSpec((None,) + w.shape[1:], lambda i: (layer, 0, 0)),
            pl.BlockSpec((tm, d), lambda i: (i, 0)),
            pl.BlockSpec((1, d), lambda i: (0, 0)),
        ],
        out_specs=pl.BlockSpec((tm, d), lambda i: (i, 0)),
        out_shape=jax.ShapeDtypeStruct((m, d), F32),
        compiler_params=_cparams(("parallel",)),
        name="out_proj_residual",
    )(ypg, ya, w, x, g)


def _ffn_kernel(x_ref, xh_ref, g1_ref, wg_ref, wv_ref, cwg_ref, cwv_ref, cbg_ref, cbv_ref, wd_ref, g2_ref,
                o_ref, h_ref, up_ref, acc_ref, *, seq):
    tm = x_ref.shape[0]
    j = pl.program_id(1)
    at_start = (pl.program_id(0) * tm) % seq == 0

    @pl.when(j == 0)
    def _():
        h_ref[0:CONV_HALO, :] = _rms(xh_ref[...], g1_ref[...]).astype(BF16)
        h_ref[CONV_HALO:, :] = _rms(x_ref[...], g1_ref[...]).astype(BF16)
        acc_ref[...] = jnp.zeros(acc_ref.shape, F32)

    def conv(w_ref, cw_ref, cb_ref):
        up = jnp.dot(h_ref[...], w_ref[...], preferred_element_type=F32)
        up_ref[0:CONV_HALO, :] = jnp.where(at_start, 0.0, up[0:CONV_HALO, :])
        up_ref[CONV_HALO:, :] = up[CONV_HALO:, :]
        out = cb_ref[...] + cw_ref[CONV_W - 1:CONV_W, :] * up[CONV_HALO:, :]
        for t in range(CONV_W - 1):
            shift = CONV_W - 1 - t
            out = out + cw_ref[t:t + 1, :] * up_ref[pl.ds(CONV_HALO - shift, tm), :]
        return out

    gate = conv(wg_ref, cwg_ref, cbg_ref)
    act = gate * (1.0 / (1.0 + jnp.exp(-gate)))
    val = conv(wv_ref, cwv_ref, cbv_ref)
    acc_ref[...] += jnp.dot((act * val).astype(BF16), wd_ref[...], preferred_element_type=F32)

    @pl.when(j == pl.num_programs(1) - 1)
    def _():
        o_ref[...] = x_ref[...] + _rms(acc_ref[...], g2_ref[...])


def _conv_ffn(x, g1, w_up, conv_w, conv_b, w_down, layer, g2, seq, tm, tf):
    m, d = x.shape
    nf = D_FF // tf
    hb = tm // CONV_HALO
    return pl.pallas_call(
        functools.partial(_ffn_kernel, seq=seq),
        grid=(m // tm, nf),
        in_specs=[
            pl.BlockSpec((tm, d), lambda i, j: (i, 0)),
            pl.BlockSpec((CONV_HALO, d), lambda i, j: (jnp.maximum(i * hb - 1, 0), 0)),
            pl.BlockSpec((1, d), lambda i, j: (0, 0)),
            pl.BlockSpec((None, d, tf), lambda i, j: (layer, 0, j)),
            pl.BlockSpec((None, d, tf), lambda i, j: (layer, 0, nf + j)),
            pl.BlockSpec((CONV_W, tf), lambda i, j: (0, j)),
            pl.BlockSpec((CONV_W, tf), lambda i, j: (0, nf + j)),
            pl.BlockSpec((1, tf), lambda i, j: (0, j)),
            pl.BlockSpec((1, tf), lambda i, j: (0, nf + j)),
            pl.BlockSpec((None, tf, d), lambda i, j: (layer, j, 0)),
            pl.BlockSpec((1, d), lambda i, j: (0, 0)),
        ],
        out_specs=pl.BlockSpec((tm, d), lambda i, j: (i, 0)),
        out_shape=jax.ShapeDtypeStruct((m, d), F32),
        scratch_shapes=[
            pltpu.VMEM((CONV_HALO + tm, d), BF16),
            pltpu.VMEM((CONV_HALO + tm, tf), F32),
            pltpu.VMEM((tm, d), F32),
        ],
        compiler_params=_cparams(("parallel", "arbitrary")),
        name="conv_ffn_residual",
    )(x, x, g1, w_up, w_up, conv_w, conv_w, conv_b, conv_b, w_down, g2)


def _split_w_in(w):
    o_q = D_POOL
    o_ki = o_q + 3 * D_ATTN + N_IDX_HEADS * D_IDX
    o_wi = o_ki + D_IDX
    o_g = o_wi + N_IDX_HEADS
    w = w.astype(BF16)
    ki = w[:, :, o_ki:o_wi]
    zk = jnp.zeros_like(ki)
    wb = jnp.concatenate([w[:, :, o_q:o_wi], zk, zk, ki], axis=2)
    zw = jnp.zeros(w.shape[:2] + (LANES - N_IDX_HEADS,), w.dtype)
    wf = jnp.concatenate([w[:, :, 0:o_q], w[:, :, o_g:], w[:, :, o_wi:o_g], zw], axis=2)
    return wb, wf


def _trunk(x, w_in, pool_w, pool_scale, sgu_ln_g, sgu_ln_b, sgu_w, sgu_b, w_out, ffn_w_up, ffn_conv_w,
           ffn_conv_b, ffn_w_down, norm_pre_mix, norm_post_mix, norm_pre_ffn, norm_post_ffn,
           *, tm, tq, kc, tf):
    batch, seq, d = x.shape
    depth = w_in.shape[0]
    xf = x.reshape(batch * seq, d)
    row = lambda v: v.reshape(1, -1)
    wb, wf = _split_w_in(w_in)
    wo = w_out.astype(BF16)
    w_up = ffn_w_up.astype(BF16)
    w_down = ffn_w_down.astype(BF16)
    pw = pool_w.astype(BF16)
    for l in range(depth):
        g = row(norm_pre_mix[l])
        zb = _norm_matmul(xf, g, wb, l, BF16, tm, ZB_WIDTH // 2)
        zf = _norm_matmul(xf, g, wf, l, F32, tm, ZF_WIDTH)
        sb_exp = jnp.broadcast_to(sgu_b[l][:, :, None], (GMLP_HEADS, CHUNK, HEAD_DIM))
        ypg = _mixer(zf, pw[l], row(pool_scale[l]), row(sgu_ln_g[l]), row(sgu_ln_b[l]), sgu_w[l], sb_exp, seq, tm)
        vt = zb[:, ZB_V * D_ATTN:(ZB_V + 1) * D_ATTN].reshape(batch * seq // kc, kc, D_ATTN).swapaxes(1, 2)
        ya = _dsa(zb, zf, vt, batch, seq, tq, kc)
        xf = _out_proj(ypg, ya, wo, l, xf, row(norm_post_mix[l]), tm)
        xf = _conv_ffn(xf, row(norm_pre_ffn[l]), w_up, ffn_conv_w[l], row(ffn_conv_b[l]), w_down, l,
                       row(norm_post_ffn[l]), seq, tm, tf)
    return xf.reshape(batch, seq, d)


def kernel(x, w_in, pool_w, pool_scale, sgu_ln_g, sgu_ln_b, sgu_w, sgu_b, w_out, ffn_w_up, ffn_conv_w,
           ffn_conv_b, ffn_w_down, norm_pre_mix, norm_post_mix, norm_pre_ffn, norm_post_ffn):
    return _trunk(x, w_in, pool_w, pool_scale, sgu_ln_g, sgu_ln_b, sgu_w, sgu_b, w_out, ffn_w_up, ffn_conv_w,
                  ffn_conv_b, ffn_w_down, norm_pre_mix, norm_post_mix, norm_pre_ffn, norm_post_ffn,
                  tm=512, tq=256, kc=512, tf=512)
```

```python
import functools

import jax
import jax.numpy as jnp
from jax import lax
from jax.experimental import pallas as pl
from jax.experimental.pallas import tpu as pltpu

F32 = jnp.float32
BF16 = jnp.bfloat16

D_MODEL = 2048
HEAD_DIM = 128
POOL_WINDOWS = (2, 4, 8, 16)
D_POOL = len(POOL_WINDOWS) * HEAD_DIM
POOL_HALO = 16
N_ATTN_HEADS = 8
D_ATTN = N_ATTN_HEADS * HEAD_DIM
N_IDX_HEADS = 16
D_IDX = 64
TOPK_MAX = 256
GMLP_HEADS = 4
D_GMLP = GMLP_HEADS * HEAD_DIM
CHUNK = 128
D_FF = 5632
CONV_W = 3
CONV_HALO = 8
EPS = 1e-6

LANES = 128
NEG_BIAS = -1e30

ZB_WIDTH = 4 * D_ATTN
ZB_Q, ZB_K, ZB_V, ZB_QI = 0, 1, 2, 3
ZF_WIDTH = D_POOL + 2 * D_GMLP + 3 * LANES
ZF_WI = (D_POOL + 2 * D_GMLP) // LANES
ZF_KA = ZF_WI + 1
ZF_KB = ZF_WI + 2

VMEM_LIMIT = 56 * 1024 * 1024


def _cparams(sem):
    return pltpu.CompilerParams(dimension_semantics=sem, vmem_limit_bytes=VMEM_LIMIT)


def _rms(x, g):
    ms = jnp.mean(x * x, axis=-1, keepdims=True)
    return x * lax.rsqrt(ms + EPS) * g


def _norm_matmul_kernel(x_ref, g_ref, w_ref, o_ref, h_ref):
    @pl.when(pl.program_id(1) == 0)
    def _():
        h_ref[...] = _rms(x_ref[...], g_ref[...]).astype(BF16)

    o_ref[...] = jnp.dot(h_ref[...], w_ref[...], preferred_element_type=F32).astype(o_ref.dtype)


def _norm_matmul(x, g, w, layer, out_dtype, tm, tn):
    m, d = x.shape
    n = w.shape[2]
    return pl.pallas_call(
        _norm_matmul_kernel,
        grid=(m // tm, n // tn),
        in_specs=[
            pl.BlockSpec((tm, d), lambda i, j: (i, 0)),
            pl.BlockSpec((1, d), lambda i, j: (0, 0)),
            pl.BlockSpec((None, d, tn), lambda i, j: (layer, 0, j)),
        ],
        out_specs=pl.BlockSpec((tm, tn), lambda i, j: (i, j)),
        out_shape=jax.ShapeDtypeStruct((m, n), out_dtype),
        scratch_shapes=[pltpu.VMEM((tm, d), BF16)],
        compiler_params=_cparams(("parallel", "arbitrary")),
        name="norm_in_proj",
    )(x, g, w)


def _gelu_tanh(x):
    return 0.5 * x * (1.0 + jnp.tanh(0.7978845608028654 * (x + 0.044715 * (x * x * x))))


def _mixer_kernel(a_ref, halo_ref, u_ref, v_ref, pw_ref, ps_ref, lg_ref, lb_ref, sw_ref, sb_ref,
                  o_ref, ext_ref, *, seq):
    tt = a_ref.shape[0]
    t0 = (pl.program_id(0) * tt) % seq
    a = a_ref[...]
    ext_ref[0:POOL_HALO, :] = jnp.where(t0 == 0, 0.0, halo_ref[...])
    ext_ref[POOL_HALO:, :] = a
    pos = (t0 + 1 + lax.broadcasted_iota(jnp.int32, (tt, 1), 0)).astype(F32)
    for g, w in enumerate(POOL_WINDOWS):
        cs = slice(g * HEAD_DIM, (g + 1) * HEAD_DIM)
        ag = a[:, cs]
        acc = ag
        for j in range(1, w):
            acc = acc + ext_ref[pl.ds(POOL_HALO - j, tt), cs]
        pooled = acc / jnp.minimum(pos, float(w)) - ag
        y = jnp.dot(pooled.astype(BF16), pw_ref[g], preferred_element_type=F32) * ps_ref[:, cs]
        o_ref[:, cs] = y.astype(o_ref.dtype)

    zu = _gelu_tanh(u_ref[...])
    zv = _gelu_tanh(v_ref[...])
    mu = jnp.mean(zv, axis=-1, keepdims=True)
    dv = zv - mu
    var = jnp.mean(dv * dv, axis=-1, keepdims=True)
    vn = (dv * lax.rsqrt(var + EPS) * lg_ref[...] + lb_ref[...]).astype(BF16)
    tril = (lax.broadcasted_iota(jnp.int32, (CHUNK, CHUNK), 0)
            >= lax.broadcasted_iota(jnp.int32, (CHUNK, CHUNK), 1))
    for g in range(GMLP_HEADS):
        cs = slice(g * HEAD_DIM, (g + 1) * HEAD_DIM)
        wg = jnp.where(tril, sw_ref[g], 0.0).astype(BF16)
        for c in range(tt // CHUNK):
            rs = slice(c * CHUNK, (c + 1) * CHUNK)
            mixed = jnp.dot(wg, vn[rs, cs], preferred_element_type=F32) + sb_ref[g]
            o_ref[rs, D_POOL + g * HEAD_DIM:D_POOL + (g + 1) * HEAD_DIM] = (zu[rs, cs] * mixed).astype(o_ref.dtype)


def _mixer(zf, pool_w, pool_scale, ln_g, ln_b, sgu_w, sgu_b_exp, seq, tt):
    m = zf.shape[0]
    hb = tt // POOL_HALO
    const3 = lambda i: (0, 0, 0)
    const2 = lambda i: (0, 0)
    return pl.pallas_call(
        functools.partial(_mixer_kernel, seq=seq),
        grid=(m // tt,),
        in_specs=[
            pl.BlockSpec((tt, D_POOL), lambda i: (i, 0)),
            pl.BlockSpec((POOL_HALO, D_POOL), lambda i: (jnp.maximum(i * hb - 1, 0), 0)),
            pl.BlockSpec((tt, D_GMLP), lambda i: (i, D_POOL // D_GMLP)),
            pl.BlockSpec((tt, D_GMLP), lambda i: (i, D_POOL // D_GMLP + 1)),
            pl.BlockSpec((len(POOL_WINDOWS), HEAD_DIM, HEAD_DIM), const3),
            pl.BlockSpec((1, D_POOL), const2),
            pl.BlockSpec((1, D_GMLP), const2),
            pl.BlockSpec((1, D_GMLP), const2),
            pl.BlockSpec((GMLP_HEADS, CHUNK, CHUNK), const3),
            pl.BlockSpec((GMLP_HEADS, CHUNK, HEAD_DIM), const3),
        ],
        out_specs=pl.BlockSpec((tt, D_POOL + D_GMLP), lambda i: (i, 0)),
        out_shape=jax.ShapeDtypeStruct((m, D_POOL + D_GMLP), BF16),
        scratch_shapes=[pltpu.VMEM((POOL_HALO + tt, D_POOL), F32)],
        compiler_params=_cparams(("parallel",)),
        name="pool_gmlp_mixer",
    )(zf, zf, zf, zf, pool_w, pool_scale, ln_g, ln_b, sgu_w, sgu_b_exp)


BISECT_CHECK_EVERY = 4
BISECT_MAX_ROUNDS = 10
SUBLANES = 8
N_PARTIAL = 4
LOG2E = 1.4426950408889634


def _fold_rows(x, op):
    rows, n = x.shape
    y = x.reshape(rows // (N_PARTIAL * SUBLANES), N_PARTIAL, SUBLANES, n)
    return op(op(y, axis=0), axis=0)


def _dsa_kernel(qi_ref, q_ref, ka_ref, kb_ref, k_ref, vt_ref, wi_ref, o_ref,
                score_ref, bias_ref, sel_ref, m_ref, l_ref, acc_ref, *, top_k, seq):
    nkc, kc, tq = score_ref.shape
    i = pl.program_id(1)
    nk = ((i + 1) * tq + kc - 1) // kc
    kf = float(top_k)
    nt = (((1,), (1,)), ((), ()))

    wts = (wi_ref[...] * (N_IDX_HEADS ** -0.5 * D_IDX ** -0.5)).T
    qpos = i * tq + lax.broadcasted_iota(jnp.int32, (1, tq), 1)

    def key_index(c, r=0, rows=None):
        return c * kc + r + lax.broadcasted_iota(jnp.int32, (kc if rows is None else rows, tq), 0)

    def score_chunk(c, carry, masked):
        mn, mx = carry
        k0 = pl.multiple_of(c * kc, kc)
        ka = ka_ref[pl.ds(k0, kc), :].astype(BF16)
        kb = kb_ref[pl.ds(k0, kc), :].astype(BF16)
        sc = jnp.zeros((kc, tq), F32)
        for p in range(N_IDX_HEADS // 2):
            qp = qi_ref[:, p * LANES:(p + 1) * LANES]
            for half, kk in enumerate((ka, kb)):
                h = 2 * p + half
                lg = lax.dot_general(kk, qp, nt, preferred_element_type=F32)
                sc = sc + jnp.maximum(lg, 0.0) * wts[h:h + 1, :]
        if masked:
            causal = key_index(c) <= qpos
            sc_hi = jnp.where(causal, sc, jnp.inf)
            sc = jnp.where(causal, sc, -jnp.inf)
        else:
            sc_hi = sc
        score_ref[c] = sc
        return jnp.minimum(mn, _fold_rows(sc_hi, jnp.min)), jnp.maximum(mx, _fold_rows(sc, jnp.max))

    ext = (jnp.full((SUBLANES, tq), jnp.inf, F32), jnp.full((SUBLANES, tq), -jnp.inf, F32))
    ext = lax.fori_loop(0, nk - 1, functools.partial(score_chunk, masked=False), ext)
    mn, mx = score_chunk(nk - 1, ext, masked=True)
    lo = jnp.min(mn, axis=0, keepdims=True)
    hi = jnp.max(mx, axis=0, keepdims=True)

    def count(pred):
        def body(c, accs):
            accs = list(accs)
            for r in range(0, kc, SUBLANES):
                hit = pred(score_ref[c, r:r + SUBLANES, :], c, r)
                accs[(r // SUBLANES) % N_PARTIAL] += jnp.where(hit, 1.0, 0.0)
            return tuple(accs)

        accs = lax.fori_loop(0, nk, body, tuple(jnp.zeros((SUBLANES, tq), F32) for _ in range(N_PARTIAL)))
        return jnp.sum(functools.reduce(lambda a, b: a + b, accs), axis=0, keepdims=True)

    c_hi = count(lambda s, c, r: s >= hi)
    c_lo = (qpos + 1).astype(F32)

    def bisect_step(_, st):
        lo, hi, c_lo, c_hi = st
        mid = 0.5 * lo + 0.5 * hi
        cnt = count(lambda s, c, r: s >= mid)
        ge = cnt >= kf
        return jnp.where(ge, mid, lo), jnp.where(ge, hi, mid), jnp.where(ge, cnt, c_lo), jnp.where(ge, c_hi, cnt)

    def bisect_round(carry):
        r, st = carry
        return r + 1, lax.fori_loop(0, BISECT_CHECK_EVERY, bisect_step, st)

    def bisect_more(carry):
        r, (lo, hi, c_lo, c_hi) = carry
        return (r < BISECT_MAX_ROUNDS) & (jnp.max(c_lo) > kf)

    _, (lo, hi, c_lo, c_hi) = lax.while_loop(bisect_more, bisect_round, (0, (lo, hi, c_lo, c_hi)))

    at_max = c_hi >= kf
    thr = jnp.where(at_max, hi, lo)
    upper = jnp.where(at_max, jnp.inf, hi)
    c_gt = jnp.where(at_max, 0.0, c_hi)
    c_thr = jnp.where(at_max, c_hi, c_lo)
    excess = c_thr > kf
    sel_ref[...] = jnp.full(sel_ref.shape, seq, jnp.int32)

    @pl.when(jnp.max(jnp.where(excess, 1.0, 0.0)) > 0.0)
    def _():
        need = kf - c_gt

        def jbisect(_, carry):
            jlo, jhi = carry
            jmid = (jlo + jhi) >> 1
            cnt = count(lambda s, c, r: (s >= thr) & (s < upper) & (key_index(c, r, SUBLANES) <= jmid))
            ge = cnt >= need
            return jnp.where(ge, jlo, jmid), jnp.where(ge, jmid, jhi)

        _, jhi = lax.fori_loop(0, seq.bit_length(), jbisect,
                               (jnp.full((1, tq), -1, jnp.int32), jnp.full((1, tq), seq - 1, jnp.int32)))
        sel_ref[...] = jnp.broadcast_to(jnp.where(excess, jhi, seq), sel_ref.shape)

    jstar = sel_ref[0:1, :]

    m_ref[...] = jnp.full(m_ref.shape, -jnp.inf, F32)
    l_ref[...] = jnp.zeros(l_ref.shape, F32)
    acc_ref[...] = jnp.zeros(acc_ref.shape, F32)

    def attend(c, _):
        k0 = pl.multiple_of(c * kc, kc)
        s_idx = score_ref[c]
        keep = (s_idx >= thr) & ((s_idx >= upper) | (key_index(c) <= jstar))
        bias_ref[...] = jnp.where(keep, 0.0, NEG_BIAS)

        def qk(h):
            cs = slice(h * HEAD_DIM, (h + 1) * HEAD_DIM)
            return lax.dot_general(k_ref[pl.ds(k0, kc), cs], q_ref[:, cs], nt, preferred_element_type=F32)

        s_next = qk(0)
        for h in range(N_ATTN_HEADS):
            cs = slice(h * HEAD_DIM, (h + 1) * HEAD_DIM)
            s = s_next
            if h + 1 < N_ATTN_HEADS:
                s_next = qk(h + 1)
            t = s * (HEAD_DIM ** -0.5 * LOG2E) + bias_ref[...]
            m_prev = m_ref[h, 0:1, :]
            m_new = jnp.maximum(m_prev, jnp.max(_fold_rows(t, jnp.max), axis=0, keepdims=True))
            alpha = jnp.exp2(m_prev - m_new)
            p = jnp.exp2(t - m_new)
            l_new = alpha * l_ref[h, 0:1, :] + jnp.sum(_fold_rows(p, jnp.sum), axis=0, keepdims=True)
            m_ref[h] = jnp.broadcast_to(m_new, (SUBLANES, tq))
            l_ref[h] = jnp.broadcast_to(l_new, (SUBLANES, tq))
            pv = jnp.dot(vt_ref[c, cs, :], p.astype(BF16), preferred_element_type=F32)
            acc_ref[cs, :] = alpha * acc_ref[cs, :] + pv
        return 0

    lax.fori_loop(0, nk, attend, 0)

    for h in range(N_ATTN_HEADS):
        cs = slice(h * HEAD_DIM, (h + 1) * HEAD_DIM)
        o_ref[:, cs] = (acc_ref[cs, :] / l_ref[h, 0:1, :]).T.astype(o_ref.dtype)


def _dsa(zb, zf, vt, batch, seq, tq, kc):
    nq = seq // tq
    nkc = seq // kc
    top_k = min(TOPK_MAX, seq // 4)
    return pl.pallas_call(
        functools.partial(_dsa_kernel, top_k=top_k, seq=seq),
        grid=(batch, nq),
        in_specs=[
            pl.BlockSpec((tq, D_ATTN), lambda b, i: (b * nq + i, ZB_QI)),
            pl.BlockSpec((tq, D_ATTN), lambda b, i: (b * nq + i, ZB_Q)),
            pl.BlockSpec((seq, LANES), lambda b, i: (b, ZF_KA)),
            pl.BlockSpec((seq, LANES), lambda b, i: (b, ZF_KB)),
            pl.BlockSpec((seq, D_ATTN), lambda b, i: (b, ZB_K)),
            pl.BlockSpec((nkc, D_ATTN, kc), lambda b, i: (b, 0, 0)),
            pl.BlockSpec((tq, LANES), lambda b, i: (b * nq + i, ZF_WI)),
        ],
        out_specs=pl.BlockSpec((tq, D_ATTN), lambda b, i: (b * nq + i, 0)),
        out_shape=jax.ShapeDtypeStruct((batch * seq, D_ATTN), BF16),
        scratch_shapes=[
            pltpu.VMEM((nkc, kc, tq), F32),
            pltpu.VMEM((kc, tq), F32),
            pltpu.VMEM((SUBLANES, tq), jnp.int32),
            pltpu.VMEM((N_ATTN_HEADS, SUBLANES, tq), F32),
            pltpu.VMEM((N_ATTN_HEADS, SUBLANES, tq), F32),
            pltpu.VMEM((D_ATTN, tq), F32),
        ],
        compiler_params=_cparams(("arbitrary", "arbitrary")),
        name="dsa_index_select_attend",
    )(zb, zb, zf, zf, zb, vt, zf)


def _out_proj_kernel(ypg_ref, ya_ref, w_ref, x_ref, g_ref, o_ref):
    mix = jnp.dot(ypg_ref[:, 0:D_POOL], w_ref[0:D_POOL, :], preferred_element_type=F32)
    mix = mix + jnp.dot(ya_ref[...], w_ref[D_POOL:D_POOL + D_ATTN, :], preferred_element_type=F32)
    mix = mix + jnp.dot(ypg_ref[:, D_POOL:], w_ref[D_POOL + D_ATTN:, :], preferred_element_type=F32)
    o_ref[...] = x_ref[...] + _rms(mix, g_ref[...])


def _out_proj(ypg, ya, w, layer, x, g, tm):
    m, d = x.shape
    return pl.pallas_call(
        _out_proj_kernel,
        grid=(m // tm,),
        in_specs=[
            pl.BlockSpec((tm, ypg.shape[1]), lambda i: (i, 0)),
            pl.BlockSpec((tm, ya.shape[1]), lambda i: (i, 0)),
            pl.BlockSpec((None,) + w.shape[1:], lambda i: (layer, 0, 0)),
            pl.BlockSpec((tm, d), lambda i: (i, 0)),
            pl.BlockSpec((1, d), lambda i: (0, 0)),
        ],
        out_specs=pl.BlockSpec((tm, d), lambda i: (i, 0)),
        out_shape=jax.ShapeDtypeStruct((m, d), F32),
        compiler_params=_cparams(("parallel",)),
        name="out_proj_residual",
    )(ypg, ya, w, x, g)


def _ffn_kernel(x_ref, xh_ref, g1_ref, wg_ref, wv_ref, cwg_ref, cwv_ref, cbg_ref, cbv_ref, wd_ref, g2_ref,
                *rest, seq, cast_next):
    if cast_next:
        up_src_ref, down_src_ref, o_ref, up_dst_ref, down_dst_ref, h_ref, up_ref, acc_ref = rest
        up_dst_ref[...] = up_src_ref[...].astype(BF16)
        down_dst_ref[...] = down_src_ref[...].astype(BF16)
    else:
        o_ref, h_ref, up_ref, acc_ref = rest
    tm = x_ref.shape[0]
    j = pl.program_id(1)
    at_start = (pl.program_id(0) * tm) % seq == 0

    @pl.when(j == 0)
    def _():
        h_ref[0:CONV_HALO, :] = _rms(xh_ref[...], g1_ref[...]).astype(BF16)
        h_ref[CONV_HALO:, :] = _rms(x_ref[...], g1_ref[...]).astype(BF16)
        acc_ref[...] = jnp.zeros(acc_ref.shape, F32)

    def conv(w_ref, cw_ref, cb_ref):
        up = jnp.dot(h_ref[...], w_ref[...], preferred_element_type=F32)
        up_ref[0:CONV_HALO, :] = jnp.where(at_start, 0.0, up[0:CONV_HALO, :])
        up_ref[CONV_HALO:, :] = up[CONV_HALO:, :]
        out = cb_ref[...] + cw_ref[CONV_W - 1:CONV_W, :] * up[CONV_HALO:, :]
        for t in range(CONV_W - 1):
            shift = CONV_W - 1 - t
            out = out + cw_ref[t:t + 1, :] * up_ref[pl.ds(CONV_HALO - shift, tm), :]
        return out

    gate = conv(wg_ref, cwg_ref, cbg_ref)
    act = gate * (1.0 / (1.0 + jnp.exp(-gate)))
    val = conv(wv_ref, cwv_ref, cbv_ref)
    acc_ref[...] += jnp.dot((act * val).astype(BF16), wd_ref[...], preferred_element_type=F32)

    @pl.when(j == pl.num_programs(1) - 1)
    def _():
        o_ref[...] = x_ref[...] + _rms(acc_ref[...], g2_ref[...])


def _conv_ffn(x, g1, w_up, conv_w, conv_b, w_down, g2, seq, tm, tf, next_f32=None):
    m, d = x.shape
    nf = D_FF // tf
    ni = m // tm
    hb = tm // CONV_HALO
    in_specs = [
        pl.BlockSpec((tm, d), lambda i, j: (i, 0)),
        pl.BlockSpec((CONV_HALO, d), lambda i, j: (jnp.maximum(i * hb - 1, 0), 0)),
        pl.BlockSpec((1, d), lambda i, j: (0, 0)),
        pl.BlockSpec((d, tf), lambda i, j: (0, j)),
        pl.BlockSpec((d, tf), lambda i, j: (0, nf + j)),
        pl.BlockSpec((CONV_W, tf), lambda i, j: (0, j)),
        pl.BlockSpec((CONV_W, tf), lambda i, j: (0, nf + j)),
        pl.BlockSpec((1, tf), lambda i, j: (0, j)),
        pl.BlockSpec((1, tf), lambda i, j: (0, nf + j)),
        pl.BlockSpec((tf, d), lambda i, j: (j, 0)),
        pl.BlockSpec((1, d), lambda i, j: (0, 0)),
    ]
    operands = [x, x, g1, w_up, w_up, conv_w, conv_w, conv_b, conv_b, w_down, g2]
    out_specs = pl.BlockSpec((tm, d), lambda i, j: (i, 0))
    out_shape = jax.ShapeDtypeStruct((m, d), F32)
    if next_f32 is not None:
        up_all, down_all, layer = next_f32
        up_blk = (d // ni, 2 * D_FF // nf)
        down_blk = (D_FF // nf, d // ni)
        in_specs += [pl.BlockSpec((None,) + up_blk, lambda i, j: (layer, i, j)),
                     pl.BlockSpec((None,) + down_blk, lambda i, j: (layer, j, i))]
        operands += [up_all, down_all]
        out_specs = [out_specs, pl.BlockSpec(up_blk, lambda i, j: (i, j)), pl.BlockSpec(down_blk, lambda i, j: (j, i))]
        out_shape = [out_shape, jax.ShapeDtypeStruct(w_up.shape, BF16), jax.ShapeDtypeStruct(w_down.shape, BF16)]
    return pl.pallas_call(
        functools.partial(_ffn_kernel, seq=seq, cast_next=next_f32 is not None),
        grid=(ni, nf),
        in_specs=in_specs,
        out_specs=out_specs,
        out_shape=out_shape,
        scratch_shapes=[
            pltpu.VMEM((CONV_HALO + tm, d), BF16),
            pltpu.VMEM((CONV_HALO + tm, tf), F32),
            pltpu.VMEM((tm, d), F32),
        ],
        compiler_params=_cparams(("parallel", "arbitrary")),
        name="conv_ffn_residual",
    )(*operands)


def _split_w_in(w):
    o_q = D_POOL
    o_ki = o_q + 3 * D_ATTN + N_IDX_HEADS * D_IDX
    o_wi = o_ki + D_IDX
    o_g = o_wi + N_IDX_HEADS
    wb = w[:, :, o_q:o_ki].astype(BF16)
    ki = w[:, :, o_ki:o_wi]
    zk = jnp.zeros_like(ki)
    zw = jnp.zeros(w.shape[:2] + (LANES - N_IDX_HEADS,), w.dtype)
    wf = jnp.concatenate([w[:, :, 0:o_q], w[:, :, o_g:], w[:, :, o_wi:o_g], zw, ki, zk, zk, ki], axis=2)
    return wb, wf.astype(BF16)


def _trunk(x, w_in, pool_w, pool_scale, sgu_ln_g, sgu_ln_b, sgu_w, sgu_b, w_out, ffn_w_up, ffn_conv_w,
           ffn_conv_b, ffn_w_down, norm_pre_mix, norm_post_mix, norm_pre_ffn, norm_post_ffn,
           *, tm, tq, kc, tf):
    batch, seq, d = x.shape
    depth = w_in.shape[0]
    xf = x.reshape(batch * seq, d)
    row = lambda v: v.reshape(1, -1)
    wb, wf = _split_w_in(w_in)
    wo = w_out.astype(BF16)
    w_up, w_down = ffn_w_up[0].astype(BF16), ffn_w_down[0].astype(BF16)
    pw = pool_w.astype(BF16)
    for l in range(depth):
        g = row(norm_pre_mix[l])
        zb = _norm_matmul(xf, g, wb, l, BF16, tm, ZB_WIDTH // 2)
        zf = _norm_matmul(xf, g, wf, l, F32, tm, ZF_WIDTH)
        sb_exp = jnp.broadcast_to(sgu_b[l][:, :, None], (GMLP_HEADS, CHUNK, HEAD_DIM))
        ypg = _mixer(zf, pw[l], row(pool_scale[l]), row(sgu_ln_g[l]), row(sgu_ln_b[l]), sgu_w[l], sb_exp, seq, tm)
        vt = zb[:, ZB_V * D_ATTN:(ZB_V + 1) * D_ATTN].reshape(batch * seq // kc, kc, D_ATTN).swapaxes(1, 2)
        ya = _dsa(zb, zf, vt, batch, seq, tq, kc)
        xf = _out_proj(ypg, ya, wo, l, xf, row(norm_post_mix[l]), tm)
        ffn_args = (xf, row(norm_pre_ffn[l]), w_up, ffn_conv_w[l], row(ffn_conv_b[l]), w_down,
                    row(norm_post_ffn[l]), seq, tm, tf)
        if l + 1 < depth:
            xf, w_up, w_down = _conv_ffn(*ffn_args, next_f32=(ffn_w_up, ffn_w_down, l + 1))
        else:
            xf = _conv_ffn(*ffn_args)
    return xf.reshape(batch, seq, d)


def kernel(x, w_in, pool_w, pool_scale, sgu_ln_g, sgu_ln_b, sgu_w, sgu_b, w_out, ffn_w_up, ffn_conv_w,
           ffn_conv_b, ffn_w_down, norm_pre_mix, norm_post_mix, norm_pre_ffn, norm_post_ffn):
    return _trunk(x, w_in, pool_w, pool_scale, sgu_ln_g, sgu_ln_b, sgu_w, sgu_b, w_out, ffn_w_up, ffn_conv_w,
                  ffn_conv_b, ffn_w_down, norm_pre_mix, norm_post_mix, norm_pre_ffn, norm_post_ffn,
                  tm=512, tq=256, kc=512, tf=512)
```

```python
import functools

import jax
import jax.numpy as jnp
from jax import lax
from jax.experimental import pallas as pl
from jax.experimental.pallas import tpu as pltpu

F32 = jnp.float32
BF16 = jnp.bfloat16

D_MODEL = 2048
HEAD_DIM = 128
POOL_WINDOWS = (2, 4, 8, 16)
D_POOL = len(POOL_WINDOWS) * HEAD_DIM
POOL_HALO = 16
N_ATTN_HEADS = 8
D_ATTN = N_ATTN_HEADS * HEAD_DIM
N_IDX_HEADS = 16
D_IDX = 64
TOPK_MAX = 256
GMLP_HEADS = 4
D_GMLP = GMLP_HEADS * HEAD_DIM
CHUNK = 128
D_FF = 5632
CONV_W = 3
CONV_HALO = 8
EPS = 1e-6

LANES = 128
NEG_BIAS = -1e30

ZB_WIDTH = 4 * D_ATTN
ZB_Q, ZB_K, ZB_V, ZB_QI = 0, 1, 2, 3
ZF_WIDTH = D_POOL + 2 * D_GMLP + 3 * LANES
ZF_WI = (D_POOL + 2 * D_GMLP) // LANES
ZF_KA = ZF_WI + 1
ZF_KB = ZF_WI + 2

VMEM_LIMIT = 56 * 1024 * 1024


def _cparams(sem):
    return pltpu.CompilerParams(dimension_semantics=sem, vmem_limit_bytes=VMEM_LIMIT)


def _rms(x, g):
    ms = jnp.mean(x * x, axis=-1, keepdims=True)
    return x * lax.rsqrt(ms + EPS) * g


def _norm_matmul_kernel(x_ref, g_ref, w_ref, o_ref, h_ref):
    @pl.when(pl.program_id(1) == 0)
    def _():
        h_ref[...] = _rms(x_ref[...], g_ref[...]).astype(BF16)

    o_ref[...] = jnp.dot(h_ref[...], w_ref[...], preferred_element_type=F32).astype(o_ref.dtype)


def _norm_matmul(x, g, w, layer, out_dtype, tm, tn):
    m, d = x.shape
    n = w.shape[2]
    return pl.pallas_call(
        _norm_matmul_kernel,
        grid=(m // tm, n // tn),
        in_specs=[
            pl.BlockSpec((tm, d), lambda i, j: (i, 0)),
            pl.BlockSpec((1, d), lambda i, j: (0, 0)),
            pl.BlockSpec((None, d, tn), lambda i, j: (layer, 0, j)),
        ],
        out_specs=pl.BlockSpec((tm, tn), lambda i, j: (i, j)),
        out_shape=jax.ShapeDtypeStruct((m, n), out_dtype),
        scratch_shapes=[pltpu.VMEM((tm, d), BF16)],
        compiler_params=_cparams(("parallel", "arbitrary")),
        name="norm_in_proj",
    )(x, g, w)


def _gelu_tanh(x):
    return 0.5 * x * (1.0 + jnp.tanh(0.7978845608028654 * (x + 0.044715 * (x * x * x))))


def _mixer_kernel(a_ref, halo_ref, u_ref, v_ref, pw_ref, ps_ref, lg_ref, lb_ref, sw_ref, sb_ref,
                  o_ref, ext_ref, *, seq):
    tt = a_ref.shape[0]
    t0 = (pl.program_id(0) * tt) % seq
    a = a_ref[...]
    ext_ref[0:POOL_HALO, :] = jnp.where(t0 == 0, 0.0, halo_ref[...])
    ext_ref[POOL_HALO:, :] = a
    pos = (t0 + 1 + lax.broadcasted_iota(jnp.int32, (tt, 1), 0)).astype(F32)
    for g, w in enumerate(POOL_WINDOWS):
        cs = slice(g * HEAD_DIM, (g + 1) * HEAD_DIM)
        ag = a[:, cs]
        acc = ag
        for j in range(1, w):
            acc = acc + ext_ref[pl.ds(POOL_HALO - j, tt), cs]
        pooled = acc / jnp.minimum(pos, float(w)) - ag
        y = jnp.dot(pooled.astype(BF16), pw_ref[g], preferred_element_type=F32) * ps_ref[:, cs]
        o_ref[:, cs] = y.astype(o_ref.dtype)

    zu = _gelu_tanh(u_ref[...])
    zv = _gelu_tanh(v_ref[...])
    mu = jnp.mean(zv, axis=-1, keepdims=True)
    dv = zv - mu
    var = jnp.mean(dv * dv, axis=-1, keepdims=True)
    vn = (dv * lax.rsqrt(var + EPS) * lg_ref[...] + lb_ref[...]).astype(BF16)
    tril = (lax.broadcasted_iota(jnp.int32, (CHUNK, CHUNK), 0)
            >= lax.broadcasted_iota(jnp.int32, (CHUNK, CHUNK), 1))
    for g in range(GMLP_HEADS):
        cs = slice(g * HEAD_DIM, (g + 1) * HEAD_DIM)
        wg = jnp.where(tril, sw_ref[g], 0.0).astype(BF16)
        for c in range(tt // CHUNK):
            rs = slice(c * CHUNK, (c + 1) * CHUNK)
            mixed = jnp.dot(wg, vn[rs, cs], preferred_element_type=F32) + sb_ref[g]
            o_ref[rs, D_POOL + g * HEAD_DIM:D_POOL + (g + 1) * HEAD_DIM] = (zu[rs, cs] * mixed).astype(o_ref.dtype)


def _mixer(zf, pool_w, pool_scale, ln_g, ln_b, sgu_w, sgu_b_exp, seq, tt):
    m = zf.shape[0]
    hb = tt // POOL_HALO
    const3 = lambda i: (0, 0, 0)
    const2 = lambda i: (0, 0)
    return pl.pallas_call(
        functools.partial(_mixer_kernel, seq=seq),
        grid=(m // tt,),
        in_specs=[
            pl.BlockSpec((tt, D_POOL), lambda i: (i, 0)),
            pl.BlockSpec((POOL_HALO, D_POOL), lambda i: (jnp.maximum(i * hb - 1, 0), 0)),
            pl.BlockSpec((tt, D_GMLP), lambda i: (i, D_POOL // D_GMLP)),
            pl.BlockSpec((tt, D_GMLP), lambda i: (i, D_POOL // D_GMLP + 1)),
            pl.BlockSpec((len(POOL_WINDOWS), HEAD_DIM, HEAD_DIM), const3),
            pl.BlockSpec((1, D_POOL), const2),
            pl.BlockSpec((1, D_GMLP), const2),
            pl.BlockSpec((1, D_GMLP), const2),
            pl.BlockSpec((GMLP_HEADS, CHUNK, CHUNK), const3),
            pl.BlockSpec((GMLP_HEADS, CHUNK, HEAD_DIM), const3),
        ],
        out_specs=pl.BlockSpec((tt, D_POOL + D_GMLP), lambda i: (i, 0)),
        out_shape=jax.ShapeDtypeStruct((m, D_POOL + D_GMLP), BF16),
        scratch_shapes=[pltpu.VMEM((POOL_HALO + tt, D_POOL), F32)],
        compiler_params=_cparams(("parallel",)),
        name="pool_gmlp_mixer",
    )(zf, zf, zf, zf, pool_w, pool_scale, ln_g, ln_b, sgu_w, sgu_b_exp)


BISECT_CHECK_EVERY = 4
BISECT_MAX_ROUNDS = 10
SUBLANES = 8
N_PARTIAL = 4
LOG2E = 1.4426950408889634


def _fold_rows(x, op):
    rows, n = x.shape
    y = x.reshape(rows // (N_PARTIAL * SUBLANES), N_PARTIAL, SUBLANES, n)
    return op(op(y, axis=0), axis=0)


def _dsa_kernel(qi_ref, q_ref, ka_ref, kb_ref, k_ref, vt_ref, wi_ref, o_ref,
                score_ref, bias_ref, sel_ref, m_ref, l_ref, acc_ref, *, top_k, seq):
    nkc, kc, tq = score_ref.shape
    i = pl.program_id(1)
    nk = ((i + 1) * tq + kc - 1) // kc
    kf = float(top_k)
    nt = (((1,), (1,)), ((), ()))

    wts = (wi_ref[...] * (N_IDX_HEADS ** -0.5 * D_IDX ** -0.5)).T
    qpos = i * tq + lax.broadcasted_iota(jnp.int32, (1, tq), 1)

    def key_index(c, r=0, rows=None):
        return c * kc + r + lax.broadcasted_iota(jnp.int32, (kc if rows is None else rows, tq), 0)

    def score_chunk(c, carry, masked):
        mn, mx = carry
        k0 = pl.multiple_of(c * kc, kc)
        ka = ka_ref[pl.ds(k0, kc), :].astype(BF16)
        kb = kb_ref[pl.ds(k0, kc), :].astype(BF16)
        sc = jnp.zeros((kc, tq), F32)
        for p in range(N_IDX_HEADS // 2):
            qp = qi_ref[:, p * LANES:(p + 1) * LANES]
            for half, kk in enumerate((ka, kb)):
                h = 2 * p + half
                lg = lax.dot_general(kk, qp, nt, preferred_element_type=F32)
                sc = sc + jnp.maximum(lg, 0.0) * wts[h:h + 1, :]
        if masked:
            causal = key_index(c) <= qpos
            sc_hi = jnp.where(causal, sc, jnp.inf)
            sc = jnp.where(causal, sc, -jnp.inf)
        else:
            sc_hi = sc
        score_ref[c] = sc
        return jnp.minimum(mn, _fold_rows(sc_hi, jnp.min)), jnp.maximum(mx, _fold_rows(sc, jnp.max))

    ext = (jnp.full((SUBLANES, tq), jnp.inf, F32), jnp.full((SUBLANES, tq), -jnp.inf, F32))
    ext = lax.fori_loop(0, nk - 1, functools.partial(score_chunk, masked=False), ext)
    mn, mx = score_chunk(nk - 1, ext, masked=True)
    lo = jnp.min(mn, axis=0, keepdims=True)
    hi = jnp.max(mx, axis=0, keepdims=True)

    def count(pred):
        def body(c, accs):
            accs = list(accs)
            for r in range(0, kc, SUBLANES):
                hit = pred(score_ref[c, r:r + SUBLANES, :], c, r)
                accs[(r // SUBLANES) % N_PARTIAL] += jnp.where(hit, 1.0, 0.0)
            return tuple(accs)

        accs = lax.fori_loop(0, nk, body, tuple(jnp.zeros((SUBLANES, tq), F32) for _ in range(N_PARTIAL)))
        return jnp.sum(functools.reduce(lambda a, b: a + b, accs), axis=0, keepdims=True)

    c_hi = count(lambda s, c, r: s >= hi)
    c_lo = (qpos + 1).astype(F32)

    def bisect_step(_, st):
        lo, hi, c_lo, c_hi = st
        mid = 0.5 * lo + 0.5 * hi
        cnt = count(lambda s, c, r: s >= mid)
        ge = cnt >= kf
        return jnp.where(ge, mid, lo), jnp.where(ge, hi, mid), jnp.where(ge, cnt, c_lo), jnp.where(ge, c_hi, cnt)

    def bisect_round(carry):
        r, st = carry
        return r + 1, lax.fori_loop(0, BISECT_CHECK_EVERY, bisect_step, st)

    def bisect_more(carry):
        r, (lo, hi, c_lo, c_hi) = carry
        return (r < BISECT_MAX_ROUNDS) & (jnp.max(c_lo) > kf)

    _, (lo, hi, c_lo, c_hi) = lax.while_loop(bisect_more, bisect_round, (0, (lo, hi, c_lo, c_hi)))

    at_max = c_hi >= kf
    thr = jnp.where(at_max, hi, lo)
    upper = jnp.where(at_max, jnp.inf, hi)
    c_gt = jnp.where(at_max, 0.0, c_hi)
    c_thr = jnp.where(at_max, c_hi, c_lo)
    excess = c_thr > kf
    sel_ref[...] = jnp.full(sel_ref.shape, seq, jnp.int32)

    @pl.when(jnp.max(jnp.where(excess, 1.0, 0.0)) > 0.0)
    def _():
        need = kf - c_gt

        def jbisect(_, carry):
            jlo, jhi = carry
            jmid = (jlo + jhi) >> 1
            cnt = count(lambda s, c, r: (s >= thr) & (s < upper) & (key_index(c, r, SUBLANES) <= jmid))
            ge = cnt >= need
            return jnp.where(ge, jlo, jmid), jnp.where(ge, jmid, jhi)

        _, jhi = lax.fori_loop(0, seq.bit_length(), jbisect,
                               (jnp.full((1, tq), -1, jnp.int32), jnp.full((1, tq), seq - 1, jnp.int32)))
        sel_ref[...] = jnp.broadcast_to(jnp.where(excess, jhi, seq), sel_ref.shape)

    jstar = sel_ref[0:1, :]

    m_ref[...] = jnp.full(m_ref.shape, -jnp.inf, F32)
    l_ref[...] = jnp.zeros(l_ref.shape, F32)
    acc_ref[...] = jnp.zeros(acc_ref.shape, F32)

    def attend(c, _):
        k0 = pl.multiple_of(c * kc, kc)
        s_idx = score_ref[c]
        keep = (s_idx >= thr) & ((s_idx >= upper) | (key_index(c) <= jstar))
        bias_ref[...] = jnp.where(keep, 0.0, NEG_BIAS)

        def qk(h):
            cs = slice(h * HEAD_DIM, (h + 1) * HEAD_DIM)
            return lax.dot_general(k_ref[pl.ds(k0, kc), cs], q_ref[:, cs], nt, preferred_element_type=F32)

        def pv(cs, alpha, p):
            upd = jnp.dot(vt_ref[c, cs, :], p, preferred_element_type=F32)
            acc_ref[cs, :] = alpha * acc_ref[cs, :] + upd

        s_next = qk(0)
        pending = None
        for h in range(N_ATTN_HEADS):
            cs = slice(h * HEAD_DIM, (h + 1) * HEAD_DIM)
            s = s_next
            if h + 1 < N_ATTN_HEADS:
                s_next = qk(h + 1)
            t = s * (HEAD_DIM ** -0.5 * LOG2E) + bias_ref[...]
            m_prev = m_ref[h, 0:1, :]
            m_new = jnp.maximum(m_prev, jnp.max(_fold_rows(t, jnp.max), axis=0, keepdims=True))
            alpha = jnp.exp2(m_prev - m_new)
            p = jnp.exp2(t - m_new)
            l_new = alpha * l_ref[h, 0:1, :] + jnp.sum(_fold_rows(p, jnp.sum), axis=0, keepdims=True)
            m_ref[h] = jnp.broadcast_to(m_new, (SUBLANES, tq))
            l_ref[h] = jnp.broadcast_to(l_new, (SUBLANES, tq))
            if pending is not None:
                pv(*pending)
            pending = (cs, alpha, p.astype(BF16))
        pv(*pending)
        return 0

    lax.fori_loop(0, nk, attend, 0)

    for h in range(N_ATTN_HEADS):
        cs = slice(h * HEAD_DIM, (h + 1) * HEAD_DIM)
        o_ref[:, cs] = (acc_ref[cs, :] / l_ref[h, 0:1, :]).T.astype(o_ref.dtype)


def _dsa(zb, zf, vt, batch, seq, tq, kc):
    nq = seq // tq
    nkc = seq // kc
    top_k = min(TOPK_MAX, seq // 4)
    return pl.pallas_call(
        functools.partial(_dsa_kernel, top_k=top_k, seq=seq),
        grid=(batch, nq),
        in_specs=[
            pl.BlockSpec((tq, D_ATTN), lambda b, i: (b * nq + i, ZB_QI)),
            pl.BlockSpec((tq, D_ATTN), lambda b, i: (b * nq + i, ZB_Q)),
            pl.BlockSpec((seq, LANES), lambda b, i: (b, ZF_KA)),
            pl.BlockSpec((seq, LANES), lambda b, i: (b, ZF_KB)),
            pl.BlockSpec((seq, D_ATTN), lambda b, i: (b, ZB_K)),
            pl.BlockSpec((nkc, D_ATTN, kc), lambda b, i: (b, 0, 0)),
            pl.BlockSpec((tq, LANES), lambda b, i: (b * nq + i, ZF_WI)),
        ],
        out_specs=pl.BlockSpec((tq, D_ATTN), lambda b, i: (b * nq + i, 0)),
        out_shape=jax.ShapeDtypeStruct((batch * seq, D_ATTN), BF16),
        scratch_shapes=[
            pltpu.VMEM((nkc, kc, tq), F32),
            pltpu.VMEM((kc, tq), F32),
            pltpu.VMEM((SUBLANES, tq), jnp.int32),
            pltpu.VMEM((N_ATTN_HEADS, SUBLANES, tq), F32),
            pltpu.VMEM((N_ATTN_HEADS, SUBLANES, tq), F32),
            pltpu.VMEM((D_ATTN, tq), F32),
        ],
        compiler_params=_cparams(("arbitrary", "arbitrary")),
        name="dsa_index_select_attend",
    )(zb, zb, zf, zf, zb, vt, zf)


def _out_proj_kernel(ypg_ref, ya_ref, w_ref, x_ref, g_ref, o_ref):
    mix = jnp.dot(ypg_ref[:, 0:D_POOL], w_ref[0:D_POOL, :], preferred_element_type=F32)
    mix = mix + jnp.dot(ya_ref[...], w_ref[D_POOL:D_POOL + D_ATTN, :], preferred_element_type=F32)
    mix = mix + jnp.dot(ypg_ref[:, D_POOL:], w_ref[D_POOL + D_ATTN:, :], preferred_element_type=F32)
    o_ref[...] = x_ref[...] + _rms(mix, g_ref[...])


def _out_proj(ypg, ya, w, layer, x, g, tm):
    m, d = x.shape
    return pl.pallas_call(
        _out_proj_kernel,
        grid=(m // tm,),
        in_specs=[
            pl.BlockSpec((tm, ypg.shape[1]), lambda i: (i, 0)),
            pl.BlockSpec((tm, ya.shape[1]), lambda i: (i, 0)),
            pl.BlockSpec((None,) + w.shape[1:], lambda i: (layer, 0, 0)),
            pl.BlockSpec((tm, d), lambda i: (i, 0)),
            pl.BlockSpec((1, d), lambda i: (0, 0)),
        ],
        out_specs=pl.BlockSpec((tm, d), lambda i: (i, 0)),
        out_shape=jax.ShapeDtypeStruct((m, d), F32),
        compiler_params=_cparams(("parallel",)),
        name="out_proj_residual",
    )(ypg, ya, w, x, g)


def _ffn_kernel(x_ref, xh_ref, g1_ref, wg_ref, wv_ref, cwg_ref, cwv_ref, cbg_ref, cbv_ref, wd_ref, g2_ref,
                *rest, seq, cast_next):
    if cast_next:
        up_src_ref, down_src_ref, o_ref, up_dst_ref, down_dst_ref, h_ref, up_ref, acc_ref = rest
        up_dst_ref[...] = up_src_ref[...].astype(BF16)
        down_dst_ref[...] = down_src_ref[...].astype(BF16)
    else:
        o_ref, h_ref, up_ref, acc_ref = rest
    tm = x_ref.shape[0]
    j = pl.program_id(1)
    at_start = (pl.program_id(0) * tm) % seq == 0

    @pl.when(j == 0)
    def _():
        h_ref[0:CONV_HALO, :] = _rms(xh_ref[...], g1_ref[...]).astype(BF16)
        h_ref[CONV_HALO:, :] = _rms(x_ref[...], g1_ref[...]).astype(BF16)
        acc_ref[...] = jnp.zeros(acc_ref.shape, F32)

    def conv(w_ref, cw_ref, cb_ref):
        up = jnp.dot(h_ref[...], w_ref[...], preferred_element_type=F32)
        up_ref[0:CONV_HALO, :] = jnp.where(at_start, 0.0, up[0:CONV_HALO, :])
        up_ref[CONV_HALO:, :] = up[CONV_HALO:, :]
        out = cb_ref[...] + cw_ref[CONV_W - 1:CONV_W, :] * up[CONV_HALO:, :]
        for t in range(CONV_W - 1):
            shift = CONV_W - 1 - t
            out = out + cw_ref[t:t + 1, :] * up_ref[pl.ds(CONV_HALO - shift, tm), :]
        return out

    gate = conv(wg_ref, cwg_ref, cbg_ref)
    act = gate * (1.0 / (1.0 + jnp.exp(-gate)))
    val = conv(wv_ref, cwv_ref, cbv_ref)
    acc_ref[...] += jnp.dot((act * val).astype(BF16), wd_ref[...], preferred_element_type=F32)

    @pl.when(j == pl.num_programs(1) - 1)
    def _():
        o_ref[...] = x_ref[...] + _rms(acc_ref[...], g2_ref[...])


def _conv_ffn(x, g1, w_up, conv_w, conv_b, w_down, g2, seq, tm, tf, next_f32=None):
    m, d = x.shape
    nf = D_FF // tf
    ni = m // tm
    hb = tm // CONV_HALO
    in_specs = [
        pl.BlockSpec((tm, d), lambda i, j: (i, 0)),
        pl.BlockSpec((CONV_HALO, d), lambda i, j: (jnp.maximum(i * hb - 1, 0), 0)),
        pl.BlockSpec((1, d), lambda i, j: (0, 0)),
        pl.BlockSpec((d, tf), lambda i, j: (0, j)),
        pl.BlockSpec((d, tf), lambda i, j: (0, nf + j)),
        pl.BlockSpec((CONV_W, tf), lambda i, j: (0, j)),
        pl.BlockSpec((CONV_W, tf), lambda i, j: (0, nf + j)),
        pl.BlockSpec((1, tf), lambda i, j: (0, j)),
        pl.BlockSpec((1, tf), lambda i, j: (0, nf + j)),
        pl.BlockSpec((tf, d), lambda i, j: (j, 0)),
        pl.BlockSpec((1, d), lambda i, j: (0, 0)),
    ]
    operands = [x, x, g1, w_up, w_up, conv_w, conv_w, conv_b, conv_b, w_down, g2]
    out_specs = pl.BlockSpec((tm, d), lambda i, j: (i, 0))
    out_shape = jax.ShapeDtypeStruct((m, d), F32)
    if next_f32 is not None:
        up_all, down_all, layer = next_f32
        up_blk = (d // ni, 2 * D_FF // nf)
        down_blk = (D_FF // nf, d // ni)
        in_specs += [pl.BlockSpec((None,) + up_blk, lambda i, j: (layer, i, j)),
                     pl.BlockSpec((None,) + down_blk, lambda i, j: (layer, j, i))]
        operands += [up_all, down_all]
        out_specs = [out_specs, pl.BlockSpec(up_blk, lambda i, j: (i, j)), pl.BlockSpec(down_blk, lambda i, j: (j, i))]
        out_shape = [out_shape, jax.ShapeDtypeStruct(w_up.shape, BF16), jax.ShapeDtypeStruct(w_down.shape, BF16)]
    return pl.pallas_call(
        functools.partial(_ffn_kernel, seq=seq, cast_next=next_f32 is not None),
        grid=(ni, nf),
        in_specs=in_specs,
        out_specs=out_specs,
        out_shape=out_shape,
        scratch_shapes=[
            pltpu.VMEM((CONV_HALO + tm, d), BF16),
            pltpu.VMEM((CONV_HALO + tm, tf), F32),
            pltpu.VMEM((tm, d), F32),
        ],
        compiler_params=_cparams(("parallel", "arbitrary")),
        name="conv_ffn_residual",
    )(*operands)


def _split_w_in(w):
    o_q = D_POOL
    o_ki = o_q + 3 * D_ATTN + N_IDX_HEADS * D_IDX
    o_wi = o_ki + D_IDX
    o_g = o_wi + N_IDX_HEADS
    wb = w[:, :, o_q:o_ki].astype(BF16)
    ki = w[:, :, o_ki:o_wi]
    zk = jnp.zeros_like(ki)
    zw = jnp.zeros(w.shape[:2] + (LANES - N_IDX_HEADS,), w.dtype)
    wf = jnp.concatenate([w[:, :, 0:o_q], w[:, :, o_g:], w[:, :, o_wi:o_g], zw, ki, zk, zk, ki], axis=2)
    return wb, wf.astype(BF16)


def _trunk(x, w_in, pool_w, pool_scale, sgu_ln_g, sgu_ln_b, sgu_w, sgu_b, w_out, ffn_w_up, ffn_conv_w,
           ffn_conv_b, ffn_w_down, norm_pre_mix, norm_post_mix, norm_pre_ffn, norm_post_ffn,
           *, tm, tq, kc, tf):
    batch, seq, d = x.shape
    depth = w_in.shape[0]
    xf = x.reshape(batch * seq, d)
    row = lambda v: v.reshape(1, -1)
    wb, wf = _split_w_in(w_in)
    wo = w_out.astype(BF16)
    w_up, w_down = ffn_w_up[0].astype(BF16), ffn_w_down[0].astype(BF16)
    pw = pool_w.astype(BF16)
    for l in range(depth):
        g = row(norm_pre_mix[l])
        zb = _norm_matmul(xf, g, wb, l, BF16, tm, ZB_WIDTH // 2)
        zf = _norm_matmul(xf, g, wf, l, F32, tm, ZF_WIDTH)
        sb_exp = jnp.broadcast_to(sgu_b[l][:, :, None], (GMLP_HEADS, CHUNK, HEAD_DIM))
        ypg = _mixer(zf, pw[l], row(pool_scale[l]), row(sgu_ln_g[l]), row(sgu_ln_b[l]), sgu_w[l], sb_exp, seq, tm)
        vt = zb[:, ZB_V * D_ATTN:(ZB_V + 1) * D_ATTN].reshape(batch * seq // kc, kc, D_ATTN).swapaxes(1, 2)
        ya = _dsa(zb, zf, vt, batch, seq, tq, kc)
        xf = _out_proj(ypg, ya, wo, l, xf, row(norm_post_mix[l]), tm)
        ffn_args = (xf, row(norm_pre_ffn[l]), w_up, ffn_conv_w[l], row(ffn_conv_b[l]), w_down,
                    row(norm_post_ffn[l]), seq, tm, tf)
        if l + 1 < depth:
            xf, w_up, w_down = _conv_ffn(*ffn_args, next_f32=(ffn_w_up, ffn_w_down, l + 1))
        else:
            xf = _conv_ffn(*ffn_args)
    return xf.reshape(batch, seq, d)


def kernel(x, w_in, pool_w, pool_scale, sgu_ln_g, sgu_ln_b, sgu_w, sgu_b, w_out, ffn_w_up, ffn_conv_w,
           ffn_conv_b, ffn_w_down, norm_pre_mix, norm_post_mix, norm_pre_ffn, norm_post_ffn):
    return _trunk(x, w_in, pool_w, pool_scale, sgu_ln_g, sgu_ln_b, sgu_w, sgu_b, w_out, ffn_w_up, ffn_conv_w,
                  ffn_conv_b, ffn_w_down, norm_pre_mix, norm_post_mix, norm_pre_ffn, norm_post_ffn,
                  tm=512, tq=256, kc=512, tf=512)
```
